```python
import jax, jax.numpy as jnp
from jax import lax
import numpy as np

D_MODEL = 1024
BATCH = 8
SEQ = 4096
DEPTH = 2
DEC_BATCH = 32
DEC_SEQ = 8
PAST_LEN = 16384
PAGE_SIZE = 128

MIX_WIDTH = D_MODEL
LRU_WIDTH = D_MODEL // 4
LRU_HEADS = 4
LRU_HEAD_DIM = LRU_WIDTH // LRU_HEADS
LRU_CONV = 4
LRU_C = 8.0
SB_HEAD_DIM = 64
SB_HEADS = D_MODEL // (2 * SB_HEAD_DIM)
SB_WIDTH = SB_HEADS * SB_HEAD_DIM
SB_BIAS_INIT = -6.0
CCM_WIDTH = MIX_WIDTH - LRU_WIDTH - SB_WIDTH
CCM_CONV = 31
D_FF = 4 * D_MODEL
Q_BLOCK = 128
ALPHA = (2 * DEPTH) ** 0.25
BETA_INIT = (8 * DEPTH) ** -0.25
NORM_EPS = 1e-5
PROJ_SPLITS = (LRU_WIDTH, 2 * LRU_WIDTH, 2 * LRU_WIDTH + SB_WIDTH, 2 * LRU_WIDTH + 2 * SB_WIDTH,
               2 * LRU_WIDTH + 3 * SB_WIDTH, 2 * LRU_WIDTH + 3 * SB_WIDTH + CCM_WIDTH)
PROJ_IN = 2 * LRU_WIDTH + 3 * SB_WIDTH + 2 * CCM_WIDTH

kernel_name = "hybrid_lru_stickbreak_conformer_step"


def _layer_norm(x, g, b):
    xf = x.astype(jnp.float32)
    mu = jnp.mean(xf, axis=-1, keepdims=True)
    var = jnp.mean(jnp.square(xf - mu), axis=-1, keepdims=True)
    return ((xf - mu) * lax.rsqrt(var + NORM_EPS) * g.astype(jnp.float32) + b.astype(jnp.float32)).astype(x.dtype)


def _rms_norm(x, g):
    xf = x.astype(jnp.float32)
    inv = lax.rsqrt(jnp.mean(jnp.square(xf), axis=-1, keepdims=True) + NORM_EPS)
    return (xf * inv * g.astype(jnp.float32)).astype(x.dtype)


def _causal_dwconv(u, buf, w, b):
    full = jnp.concatenate([buf, u], axis=1)
    y = lax.conv_general_dilated(full, w[:, None, :], window_strides=(1,), padding="VALID",
                                 dimension_numbers=("NWC", "WIO", "NWC"),
                                 feature_group_count=u.shape[-1])
    return y + b, full[:, full.shape[1] - (w.shape[0] - 1):]


def _rg_lru(xc, h0, wa, ba, wx, bx, lam):
    bsz, t, c = xc.shape
    xh = xc.reshape(bsz, t, LRU_HEADS, LRU_HEAD_DIM)
    gate_r = jnp.einsum("bthi,hij->bthj", xh, wa).reshape(bsz, t, c) + ba
    gate_i = jnp.einsum("bthi,hij->bthj", xh, wx).reshape(bsz, t, c) + bx
    log_a = -LRU_C * jax.nn.sigmoid(gate_r.astype(jnp.float32)) * jax.nn.softplus(-lam.astype(jnp.float32))
    a = jnp.exp(log_a)
    b = jnp.sqrt(-jnp.expm1(2.0 * log_a)) * jax.nn.sigmoid(gate_i.astype(jnp.float32)) * xc.astype(jnp.float32)

    def step(h, ab):
        h = ab[0] * h + ab[1]
        return h, h

    h_last, hs = lax.scan(step, h0.astype(jnp.float32), (jnp.swapaxes(a, 0, 1), jnp.swapaxes(b, 0, 1)))
    return jnp.swapaxes(hs, 0, 1).astype(xc.dtype), h_last.astype(xc.dtype)


def _stick_breaking_weights(z, valid):
    log_stay = jnp.where(valid, jax.nn.log_sigmoid(-z), 0.0)
    later = lax.cumsum(log_stay, axis=z.ndim - 1, reverse=True) - log_stay
    return jnp.where(valid, jnp.exp(jax.nn.log_sigmoid(z) + later), 0.0)


def _sb_attention_prompt(q, k, v, bias):
    bsz, t, h, dh = q.shape
    n_blocks = t // Q_BLOCK
    q_blocks = jnp.swapaxes(q.reshape(bsz, n_blocks, Q_BLOCK, h, dh), 0, 1)
    k_pos = jnp.arange(t)
    bias_f = bias.astype(jnp.float32)[:, None, None]

    def one_block(args):
        qb, bi = args
        z = jnp.einsum("bqhd,bkhd->bhqk", qb, k, preferred_element_type=jnp.float32) * (dh ** -0.5) + bias_f
        q_pos = bi * Q_BLOCK + jnp.arange(Q_BLOCK)
        w = _stick_breaking_weights(z, k_pos[None, :] < q_pos[:, None])
        return jnp.einsum("bhqk,bkhd->bqhd", w.astype(v.dtype), v)

    o = lax.map(one_block, (q_blocks, jnp.arange(n_blocks)))
    return jnp.swapaxes(o, 0, 1).reshape(bsz, t, h, dh)


def _sb_attention_sample(q, k_new, v_new, k_past, v_past, bias):
    t = q.shape[1]
    p = k_past.shape[1]
    dh = q.shape[-1]
    z = jnp.concatenate([
        jnp.einsum("bqhd,bkhd->bhqk", q, k_past, preferred_element_type=jnp.float32),
        jnp.einsum("bqhd,bkhd->bhqk", q, k_new, preferred_element_type=jnp.float32)], axis=-1) * (dh ** -0.5)
    z = z + bias.astype(jnp.float32)[:, None, None]
    q_pos = p + jnp.arange(t)
    k_pos = jnp.arange(p + t)
    w = _stick_breaking_weights(z, k_pos[None, :] < q_pos[:, None]).astype(v_new.dtype)
    return (jnp.einsum("bhqk,bkhd->bqhd", w[..., :p], v_past)
            + jnp.einsum("bhqk,bkhd->bqhd", w[..., p:], v_new))


def _hybrid_mixer(x, lru_buf, lru_h, ccm_buf, past_kv, lp):
    (w_in, sb_bias, lru_conv_w, lru_conv_b, lru_wa, lru_ba, lru_wx, lru_bx, lru_lam,
     ccm_conv_w, ccm_conv_b, ccm_ln_g, ccm_ln_b, g_mix, w_out) = lp[:15]
    bsz, t, _ = x.shape
    proj = jnp.einsum("btd,dp->btp", x, w_in)
    xr, gr, q, k, v, ga, gb = jnp.split(proj, list(PROJ_SPLITS), axis=-1)
    xc, lru_buf_new = _causal_dwconv(xr, lru_buf, lru_conv_w, lru_conv_b)
    hs, lru_h_new = _rg_lru(xc, lru_h, lru_wa, lru_ba, lru_wx, lru_bx, lru_lam)
    y_a = hs * jax.nn.gelu(gr)
    q = q.reshape(bsz, t, SB_HEADS, SB_HEAD_DIM)
    k = k.reshape(bsz, t, SB_HEADS, SB_HEAD_DIM)
    v = v.reshape(bsz, t, SB_HEADS, SB_HEAD_DIM)
    if past_kv is None:
        o = _sb_attention_prompt(q, k, v, sb_bias)
    else:
        o = _sb_attention_sample(q, k, v, past_kv[0], past_kv[1], sb_bias)
    y_b = o.reshape(bsz, t, SB_WIDTH)
    u = ga * jax.nn.sigmoid(gb)
    uc, ccm_buf_new = _causal_dwconv(u, ccm_buf, ccm_conv_w, ccm_conv_b)
    y_c = jax.nn.silu(_layer_norm(uc, ccm_ln_g, ccm_ln_b))
    g_a, g_b, g_c = jnp.split(g_mix, [LRU_WIDTH, LRU_WIDTH + SB_WIDTH])
    y = jnp.concatenate([_rms_norm(y_a, g_a), _rms_norm(y_b, g_b), _rms_norm(y_c, g_c)], axis=-1)
    out = jnp.einsum("btm,md->btd", y, w_out)
    return out, (k, v, lru_buf_new, lru_h_new, ccm_buf_new)


def _decoder_layer(x, lru_buf, lru_h, ccm_buf, past_kv, lp):
    ln1_g, ln1_b, w_up, w_down, ln2_g, ln2_b = lp[15:]
    m, new_state = _hybrid_mixer(x, lru_buf, lru_h, ccm_buf, past_kv, lp)
    x = _layer_norm(ALPHA * x + m, ln1_g, ln1_b)
    hdn = jnp.square(jax.nn.relu(jnp.einsum("btd,df->btf", x, w_up)))
    x = _layer_norm(ALPHA * x + jnp.einsum("btf,fd->btd", hdn, w_down), ln2_g, ln2_b)
    return x, new_state


def setup_inputs(seed: int = 0) -> dict:
    key = jax.random.key(seed)
    ks = jax.random.split(key, 32)
    n_pages = PAST_LEN // PAGE_SIZE
    n_pool = (DEC_BATCH * n_pages * 5) // 4
    nrm = jax.random.normal
    page_table = jax.random.permutation(ks[3], n_pool)[:DEC_BATCH * n_pages].reshape(DEC_BATCH, n_pages).astype(jnp.int32)
    a0 = jax.random.uniform(ks[14], (DEPTH, LRU_WIDTH), minval=0.9, maxval=0.999)
    return {
        "x_prompt": nrm(ks[0], (BATCH, SEQ, D_MODEL), jnp.float32),
        "x_sample": nrm(ks[1], (DEC_BATCH, DEC_SEQ, D_MODEL), jnp.float32),
        "cache_k": nrm(ks[2], (DEPTH, n_pool, PAGE_SIZE, SB_HEADS, SB_HEAD_DIM), jnp.float32),
        "cache_v": nrm(ks[4], (DEPTH, n_pool, PAGE_SIZE, SB_HEADS, SB_HEAD_DIM), jnp.float32),
        "page_table": page_table,
        "state_lru_conv": nrm(ks[5], (DEPTH, DEC_BATCH, LRU_CONV - 1, LRU_WIDTH), jnp.float32),
        "state_lru_h": 0.5 * nrm(ks[6], (DEPTH, DEC_BATCH, LRU_WIDTH), jnp.float32),
        "state_ccm_conv": nrm(ks[7], (DEPTH, DEC_BATCH, CCM_CONV - 1, CCM_WIDTH), jnp.float32),
        "w_in": nrm(ks[8], (DEPTH, D_MODEL, PROJ_IN), jnp.float32) * D_MODEL ** -0.5,
        "sb_bias": SB_BIAS_INIT + 0.1 * nrm(ks[28], (DEPTH, SB_HEADS), jnp.float32),
        "lru_conv_w": nrm(ks[9], (DEPTH, LRU_CONV, LRU_WIDTH), jnp.float32) * LRU_CONV ** -0.5,
        "lru_conv_b": 0.01 * nrm(ks[10], (DEPTH, LRU_WIDTH), jnp.float32),
        "lru_wa": nrm(ks[11], (DEPTH, LRU_HEADS, LRU_HEAD_DIM, LRU_HEAD_DIM), jnp.float32) * LRU_HEAD_DIM ** -0.5,
        "lru_ba": 0.1 * nrm(ks[12], (DEPTH, LRU_WIDTH), jnp.float32),
        "lru_wx": nrm(ks[13], (DEPTH, LRU_HEADS, LRU_HEAD_DIM, LRU_HEAD_DIM), jnp.float32) * LRU_HEAD_DIM ** -0.5,
        "lru_bx": 0.1 * nrm(ks[15], (DEPTH, LRU_WIDTH), jnp.float32),
        "lru_lam": jnp.log(a0) - jnp.log1p(-a0),
        "ccm_conv_w": nrm(ks[16], (DEPTH, CCM_CONV, CCM_WIDTH), jnp.float32) * CCM_CONV ** -0.5,
        "ccm_conv_b": 0.01 * nrm(ks[17], (DEPTH, CCM_WIDTH), jnp.float32),
        "ccm_ln_g": 1.0 + 0.1 * nrm(ks[18], (DEPTH, CCM_WIDTH), jnp.float32),
        "ccm_ln_b": 0.01 * nrm(ks[19], (DEPTH, CCM_WIDTH), jnp.float32),
        "g_mix": 1.0 + 0.1 * nrm(ks[20], (DEPTH, MIX_WIDTH), jnp.float32),
        "w_out": nrm(ks[21], (DEPTH, MIX_WIDTH, D_MODEL), jnp.float32) * (MIX_WIDTH ** -0.5 * BETA_INIT),
        "ln1_g": 1.0 + 0.1 * nrm(ks[22], (DEPTH, D_MODEL), jnp.float32),
        "ln1_b": 0.01 * nrm(ks[23], (DEPTH, D_MODEL), jnp.float32),
        "w_up": nrm(ks[24], (DEPTH, D_MODEL, D_FF), jnp.float32) * D_MODEL ** -0.5,
        "w_down": nrm(ks[25], (DEPTH, D_FF, D_MODEL), jnp.float32) * (D_FF ** -0.5 * BETA_INIT),
        "ln2_g": 1.0 + 0.1 * nrm(ks[26], (DEPTH, D_MODEL), jnp.float32),
        "ln2_b": 0.01 * nrm(ks[27], (DEPTH, D_MODEL), jnp.float32),
    }


def reference(x_prompt, x_sample, cache_k, cache_v, page_table, state_lru_conv, state_lru_h, state_ccm_conv,
              w_in, sb_bias, lru_conv_w, lru_conv_b, lru_wa, lru_ba, lru_wx, lru_bx, lru_lam,
              ccm_conv_w, ccm_conv_b, ccm_ln_g, ccm_ln_b, g_mix, w_out,
              ln1_g, ln1_b, w_up, w_down, ln2_g, ln2_b):
    bsz = x_prompt.shape[0]
    dec_b = x_sample.shape[0]
    dt = x_prompt.dtype
    zero_lru_buf = jnp.zeros((bsz, LRU_CONV - 1, LRU_WIDTH), dt)
    zero_lru_h = jnp.zeros((bsz, LRU_WIDTH), dt)
    zero_ccm_buf = jnp.zeros((bsz, CCM_CONV - 1, CCM_WIDTH), dt)
    xp, xs = x_prompt, x_sample
    kp, vp, lbp, lhp, cbp = [], [], [], [], []
    ksm, vsm, lbs, lhs, cbs = [], [], [], [], []
    for l in range(DEPTH):
        lp = (w_in[l], sb_bias[l], lru_conv_w[l], lru_conv_b[l], lru_wa[l], lru_ba[l], lru_wx[l], lru_bx[l],
              lru_lam[l], ccm_conv_w[l], ccm_conv_b[l], ccm_ln_g[l], ccm_ln_b[l], g_mix[l], w_out[l],
              ln1_g[l], ln1_b[l], w_up[l], w_down[l], ln2_g[l], ln2_b[l])
        xp, (k_l, v_l, lb_l, lh_l, cb_l) = _decoder_layer(xp, zero_lru_buf, zero_lru_h, zero_ccm_buf, None, lp)
        kp.append(k_l); vp.append(v_l); lbp.append(lb_l); lhp.append(lh_l); cbp.append(cb_l)
        k_past = jnp.take(cache_k[l], page_table, axis=0).reshape(dec_b, -1, SB_HEADS, SB_HEAD_DIM)
        v_past = jnp.take(cache_v[l], page_table, axis=0).reshape(dec_b, -1, SB_HEADS, SB_HEAD_DIM)
        xs, (k_l, v_l, lb_l, lh_l, cb_l) = _decoder_layer(xs, state_lru_conv[l], state_lru_h[l], state_ccm_conv[l],
                                                          (k_past, v_past), lp)
        ksm.append(k_l); vsm.append(v_l); lbs.append(lb_l); lhs.append(lh_l); cbs.append(cb_l)
    return (xp, xs,
            jnp.stack(kp), jnp.stack(vp), jnp.stack(lbp), jnp.stack(lhp), jnp.stack(cbp),
            jnp.stack(ksm), jnp.stack(vsm), jnp.stack(lbs), jnp.stack(lhs), jnp.stack(cbs))
```

```python
import functools

import jax
import jax.numpy as jnp
from jax import lax
from jax.experimental import pallas as pl
from jax.experimental.pallas import tpu as pltpu

F32 = jnp.float32
BF16 = jnp.bfloat16

LRU_HEADS = 4
LRU_CONV = 4
LRU_C = 8.0
SB_HEAD_DIM = 64
CCM_CONV = 31
NORM_EPS = 1e-5
PAGE_SIZE = 128
LANES = 128
SUBLANES = 8
VMEM_LIMIT = 48 * 1024 * 1024


def _params(sem):
    return pltpu.CompilerParams(dimension_semantics=sem, vmem_limit_bytes=VMEM_LIMIT)


def _sigmoid(x):
    return 1.0 / (1.0 + jnp.exp(-x))


def _softplus(x):
    return jnp.maximum(x, 0.0) + jnp.log1p(jnp.exp(-jnp.abs(x)))


def _layer_norm(x, g, b):
    mu = jnp.mean(x, axis=-1, keepdims=True)
    xc = x - mu
    var = jnp.mean(xc * xc, axis=-1, keepdims=True)
    return xc * lax.rsqrt(var + NORM_EPS) * g + b


def _rms_norm(x, g):
    return x * lax.rsqrt(jnp.mean(x * x, axis=-1, keepdims=True) + NORM_EPS) * g


def _in_proj_kernel(x_ref, w_ref, pa_ref, qkv_ref, k_ref, v_ref, pc_ref, *, lru_w, sb_w, kv_transposed):
    p = jnp.dot(x_ref[...].astype(BF16), w_ref[...], preferred_element_type=F32)
    a_end = 2 * lru_w
    q_end = a_end + sb_w
    k_end = q_end + sb_w
    v_end = k_end + sb_w
    pa_ref[...] = p[:, :a_end]
    q = p[:, a_end:q_end] * (SB_HEAD_DIM ** -0.5)
    k = p[:, q_end:k_end]
    v = p[:, k_end:v_end]
    qkv_ref[:, :sb_w] = q.astype(BF16)
    qkv_ref[:, sb_w:2 * sb_w] = k.astype(BF16)
    qkv_ref[:, 2 * sb_w:] = v.astype(BF16)
    k_ref[...] = k.T if kv_transposed else k
    v_ref[...] = v.T if kv_transposed else v
    pc_ref[...] = p[:, v_end:]


def _in_proj(x3, w_bf16, lru_w, sb_w, ccm_w, tm, kv_transposed):
    g, r, d = x3.shape
    pw = w_bf16.shape[1]
    row = lambda n: pl.BlockSpec((None, tm, n), lambda b, i: (b, i, 0))
    if kv_transposed:
        kv_spec = pl.BlockSpec((None, sb_w, tm), lambda b, i: (b, 0, i))
        kv_shape = jax.ShapeDtypeStruct((g, sb_w, r), F32)
    else:
        kv_spec = row(sb_w)
        kv_shape = jax.ShapeDtypeStruct((g, r, sb_w), F32)
    return pl.pallas_call(
        functools.partial(_in_proj_kernel, lru_w=lru_w, sb_w=sb_w, kv_transposed=kv_transposed),
        grid=(g, r // tm),
        in_specs=[row(d), pl.BlockSpec((d, pw), lambda b, i: (0, 0))],
        out_specs=[row(2 * lru_w), row(3 * sb_w), kv_spec, kv_spec, row(2 * ccm_w)],
        out_shape=[jax.ShapeDtypeStruct((g, r, 2 * lru_w), F32),
                   jax.ShapeDtypeStruct((g, r, 3 * sb_w), BF16),
                   kv_shape, kv_shape,
                   jax.ShapeDtypeStruct((g, r, 2 * ccm_w), F32)],
        compiler_params=_params(("parallel", "parallel")),
        name="in_proj",
    )(x3, w_bf16)


def _shift_rows(x, s, fill, row):
    return jnp.where(row >= s, pltpu.roll(x, s, axis=0), fill)


def _lru_kernel(pa_ref, buf0_ref, h0_ref, cw_ref, cb_ref, wg_ref, bg_ref, lam_ref, g_ref,
                ya_ref, bufn_ref, hn_ref, xbuf, hcar, *, tc, width):
    t = pl.program_id(1)
    keep = LRU_CONV - 1
    base = SUBLANES - keep

    @pl.when(t == 0)
    def _():
        xbuf[base:SUBLANES, :] = buf0_ref[...]
        hcar[...] = h0_ref[...]

    xr = pa_ref[:, :width]
    gr = pa_ref[:, width:]
    xbuf[SUBLANES:SUBLANES + tc, :] = xr
    xc = cb_ref[...] + cw_ref[0:1, :] * xbuf[base:base + tc, :]
    for k in range(1, LRU_CONV):
        xc = xc + cw_ref[k:k + 1, :] * xbuf[base + k:base + k + tc, :]
    tail = xbuf[base + tc:SUBLANES + tc, :]
    xbuf[base:SUBLANES, :] = tail
    bufn_ref[...] = tail

    gates = jnp.dot(xc.astype(BF16), wg_ref[...], preferred_element_type=F32) + bg_ref[...]
    log_a = (-LRU_C) * _sigmoid(gates[:, :width]) * _softplus(-lam_ref[...])
    a = jnp.exp(log_a)
    th = jnp.tanh(log_a)
    b = jnp.sqrt(-2.0 * th / (1.0 - th)) * _sigmoid(gates[:, width:]) * xc

    row = lax.broadcasted_iota(jnp.int32, (tc, width), 0)
    s = 1
    while s < tc:
        b = a * _shift_rows(b, s, 0.0, row) + b
        a = a * _shift_rows(a, s, 1.0, row)
        s *= 2
    h = a * hcar[...] + b
    hcar[...] = h[tc - 1:tc, :]
    hn_ref[...] = h[tc - 1:tc, :]

    gelu = 0.5 * gr * (1.0 + jnp.tanh(0.7978845608028654 * (gr + 0.044715 * gr * gr * gr)))
    ya_ref[...] = _rms_norm(h * gelu, g_ref[...])


def _lru(pa3, buf0, h0, cw, cb, wg, bg, lam, g_a, tc):
    bsz, t, w2 = pa3.shape
    width = w2 // 2
    keep = LRU_CONV - 1
    const = lambda shp: pl.BlockSpec(shp, lambda b, i: (0,) * len(shp))
    return pl.pallas_call(
        functools.partial(_lru_kernel, tc=tc, width=width),
        grid=(bsz, t // tc),
        in_specs=[pl.BlockSpec((None, tc, w2), lambda b, i: (b, i, 0)),
                  pl.BlockSpec((None, keep, width), lambda b, i: (b, 0, 0)),
                  pl.BlockSpec((None, 1, width), lambda b, i: (b, 0, 0)),
                  const((LRU_CONV, width)), const((1, width)), const((width, 2 * width)),
                  const((1, 2 * width)), const((1, width)), const((1, width))],
        out_specs=[pl.BlockSpec((None, tc, width), lambda b, i: (b, i, 0)),
                   pl.BlockSpec((None, keep, width), lambda b, i: (b, 0, 0)),
                   pl.BlockSpec((None, 1, width), lambda b, i: (b, 0, 0))],
        out_shape=[jax.ShapeDtypeStruct((bsz, t, width), F32),
                   jax.ShapeDtypeStruct((bsz, keep, width), F32),
                   jax.ShapeDtypeStruct((bsz, 1, width), F32)],
        scratch_shapes=[pltpu.VMEM((SUBLANES + tc, width), F32), pltpu.VMEM((1, width), F32)],
        compiler_params=_params(("parallel", "arbitrary")),
        name="lru",
    )(pa3, buf0, h0, cw, cb, wg, bg, lam, g_a)


def _ccm_kernel(pc_ref, buf0_ref, cw_ref, cb_ref, lg_ref, lb_ref, g_ref,
                yc_ref, bufn_ref, ubuf, *, tc, width):
    t = pl.program_id(1)
    keep = CCM_CONV - 1
    pad = 4 * SUBLANES
    base = pad - keep

    @pl.when(t == 0)
    def _():
        ubuf[base:pad, :] = buf0_ref[...]

    u = pc_ref[:, :width] * _sigmoid(pc_ref[:, width:])
    ubuf[pad:pad + tc, :] = u
    acc = cb_ref[...] + cw_ref[0:1, :] * ubuf[base:base + tc, :]
    for k in range(1, CCM_CONV):
        acc = acc + cw_ref[k:k + 1, :] * ubuf[base + k:base + k + tc, :]
    tail = ubuf[base + tc:pad + tc, :]
    ubuf[base:pad, :] = tail
    bufn_ref[...] = tail

    y = _layer_norm(acc, lg_ref[...], lb_ref[...])
    y = y * _sigmoid(y)
    yc_ref[...] = _rms_norm(y, g_ref[...])


def _ccm(pc3, buf0, cw, cb, lg, lb, g_c, tc):
    bsz, t, w2 = pc3.shape
    width = w2 // 2
    keep = CCM_CONV - 1
    const = lambda shp: pl.BlockSpec(shp, lambda b, i: (0,) * len(shp))
    return pl.pallas_call(
        functools.partial(_ccm_kernel, tc=tc, width=width),
        grid=(bsz, t // tc),
        in_specs=[pl.BlockSpec((None, tc, w2), lambda b, i: (b, i, 0)),
                  pl.BlockSpec((None, keep, width), lambda b, i: (b, 0, 0)),
                  const((CCM_CONV, width)), const((1, width)), const((1, width)),
                  const((1, width)), const((1, width))],
        out_specs=[pl.BlockSpec((None, tc, width), lambda b, i: (b, i, 0)),
                   pl.BlockSpec((None, keep, width), lambda b, i: (b, 0, 0))],
        out_shape=[jax.ShapeDtypeStruct((bsz, t, width), F32),
                   jax.ShapeDtypeStruct((bsz, keep, width), F32)],
        scratch_shapes=[pltpu.VMEM((4 * SUBLANES + tc, width), F32)],
        compiler_params=_params(("parallel", "arbitrary")),
        name="ccm",
    )(pc3, buf0, cw, cb, lg, lb, g_c)


def _sb_prompt_kernel(bias_ref, q_ref, k_ref, v_ref, o_ref, acc_ref, r_ref, *, tq, tk):
    hp = pl.program_id(1)
    i = pl.program_id(2)
    rows = 2 * tq
    lane = lax.broadcasted_iota(jnp.int32, (tq, LANES), 1)
    qb = q_ref[...]
    zero = jnp.zeros_like(qb)
    q2 = jnp.concatenate([jnp.where(lane < SB_HEAD_DIM, qb, zero),
                          jnp.where(lane >= SB_HEAD_DIM, qb, zero)], axis=0)
    rowi = lax.broadcasted_iota(jnp.int32, (rows, 1), 0)
    bias = jnp.where(rowi < tq, bias_ref[2 * hp], bias_ref[2 * hp + 1])
    kk = lax.broadcasted_iota(jnp.int32, (tk, tk), 0)
    ss = lax.broadcasted_iota(jnp.int32, (tk, tk), 1)
    tri = jnp.where(kk >= ss, 1.0, 0.0).astype(BF16)

    acc_ref[...] = jnp.zeros_like(acc_ref)
    r_ref[...] = jnp.zeros_like(r_ref)

    def block(j, masked):
        off = pl.multiple_of(j * tk, tk)
        kb = k_ref[pl.ds(off, tk), :]
        vb = v_ref[pl.ds(off, tk), :]
        z = lax.dot_general(q2, kb, (((1,), (1,)), ((), ())), preferred_element_type=F32) + bias
        sp = _softplus(z)
        if masked:
            qpos = i * tq + jnp.where(rowi < tq, rowi, rowi - tq)
            kpos = j * tk + lax.broadcasted_iota(jnp.int32, (rows, tk), 1)
            valid = kpos < qpos
            sp = jnp.where(valid, sp, 0.0)
        c = jnp.dot(sp.astype(BF16), tri, preferred_element_type=F32)
        w = jnp.exp(z - (c + r_ref[...]))
        if masked:
            w = jnp.where(valid, w, 0.0)
        r_ref[...] = r_ref[...] + c[:, 0:1]
        acc_ref[...] += jnp.dot(w.astype(BF16), vb, preferred_element_type=F32)

    jd = (i * tq) // tk
    block(jd, True)

    def body(n, carry):
        block(jd - 1 - n, False)
        return carry

    lax.fori_loop(0, jd, body, 0)
    o_ref[...] = jnp.where(lane < SB_HEAD_DIM, acc_ref[:tq, :], acc_ref[tq:, :])


def _sb_prompt(qkv3, bias, sb_w, tq, tk):
    bsz, t, _ = qkv3.shape
    npair = sb_w // LANES
    return pl.pallas_call(
        functools.partial(_sb_prompt_kernel, tq=tq, tk=tk),
        grid_spec=pltpu.PrefetchScalarGridSpec(
            num_scalar_prefetch=0,
            grid=(bsz, npair, t // tq),
            in_specs=[pl.BlockSpec(memory_space=pltpu.SMEM),
                      pl.BlockSpec((None, tq, LANES), lambda b, h, i: (b, i, h)),
                      pl.BlockSpec((None, t, LANES), lambda b, h, i: (b, 0, npair + h)),
                      pl.BlockSpec((None, t, LANES), lambda b, h, i: (b, 0, 2 * npair + h))],
            out_specs=pl.BlockSpec((None, tq, LANES), lambda b, h, i: (b, i, h)),
            scratch_shapes=[pltpu.VMEM((2 * tq, LANES), F32), pltpu.VMEM((2 * tq, 1), F32)]),
        out_shape=jax.ShapeDtypeStruct((bsz, t, sb_w), F32),
        compiler_params=_params(("parallel", "parallel", "arbitrary")),
        name="sb_prompt",
    )(bias, qkv3, qkv3, qkv3)


def _sb_sample_kernel(pt_ref, bias_ref, q_ref, kn_ref, vn_ref, *rest, n_heads, tn, gp):
    del pt_ref
    k_refs = rest[:gp]
    v_refs = rest[gp:2 * gp]
    o_ref, qbd_ref, acc_ref, r_ref = rest[2 * gp:]
    j = pl.program_id(1)
    rows = n_heads * tn
    width = n_heads * SB_HEAD_DIM
    rowi = lax.broadcasted_iota(jnp.int32, (rows, 1), 0)
    row_head = lax.shift_right_logical(rowi, tn.bit_length() - 1)
    row_tok = jnp.bitwise_and(rowi, tn - 1)
    dim_shift = SB_HEAD_DIM.bit_length() - 1
    kk = lax.broadcasted_iota(jnp.int32, (PAGE_SIZE, PAGE_SIZE), 0)
    ss = lax.broadcasted_iota(jnp.int32, (PAGE_SIZE, PAGE_SIZE), 1)
    tri = jnp.where(kk >= ss, 1.0, 0.0).astype(BF16)
    bias = jnp.zeros((rows, 1), F32)
    for h in range(n_heads):
        bias = jnp.where(row_head == h, bias_ref[h], bias)

    def block(kt, vt, valid):
        z = jnp.dot(qbd_ref[...], kt, preferred_element_type=F32) + bias
        sp = _softplus(z)
        if valid is not None:
            sp = jnp.where(valid, sp, 0.0)
        c = jnp.dot(sp.astype(BF16), tri, preferred_element_type=F32)
        w = jnp.exp(z - (c + r_ref[...]))
        if valid is not None:
            w = jnp.where(valid, w, 0.0)
        r_ref[...] = r_ref[...] + c[:, 0:1]
        acc_ref[...] += lax.dot_general(w.astype(BF16), vt, (((1,), (1,)), ((), ())),
                                        preferred_element_type=F32)

    @pl.when(j == 0)
    def _():
        q = q_ref[...].astype(F32)
        qt = jnp.concatenate([q] * n_heads, axis=0)
        col = lax.broadcasted_iota(jnp.int32, (rows, width), 1)
        qbd_ref[...] = jnp.where(lax.shift_right_logical(col, dim_shift) == row_head,
                                 qt, 0.0).astype(BF16)
        acc_ref[...] = jnp.zeros_like(acc_ref)
        r_ref[...] = jnp.zeros_like(r_ref)
        pad = jnp.zeros((PAGE_SIZE - tn, width), F32)
        kn = jnp.concatenate([kn_ref[...], pad], axis=0).T.astype(BF16)
        vn = jnp.concatenate([vn_ref[...], pad], axis=0).T.astype(BF16)
        kpos = lax.broadcasted_iota(jnp.int32, (rows, PAGE_SIZE), 1)
        block(kn, vn, kpos < row_tok)

    for g in range(gp - 1, -1, -1):
        block(k_refs[g][...].astype(BF16), v_refs[g][...].astype(BF16), None)

    @pl.when(j == pl.num_programs(1) - 1)
    def _():
        col = lax.broadcasted_iota(jnp.int32, (tn, width), 1)
        out = jnp.zeros((tn, width), F32)
        for h in range(n_heads):
            out = out + jnp.where(lax.shift_right_logical(col, dim_shift) == h,
                                  acc_ref[h * tn:(h + 1) * tn, :], 0.0)
        o_ref[...] = out


def _sb_sample(qkv3, k3, v3, cache_kt, cache_vt, page_table, bias, layer, gp):
    bsz, tn, w3 = qkv3.shape
    width = w3 // 3
    n_heads = width // SB_HEAD_DIM
    n_pages = page_table.shape[1]
    ng = n_pages // gp

    def page_spec(g):
        return pl.BlockSpec((None, None, width, PAGE_SIZE),
                            lambda b, j, pt, g=g: (layer, pt[b, (ng - 1 - j) * gp + g], 0, 0))

    tok = lambda n: pl.BlockSpec((None, tn, n), lambda b, j, pt: (b, 0, 0))
    return pl.pallas_call(
        functools.partial(_sb_sample_kernel, n_heads=n_heads, tn=tn, gp=gp),
        grid_spec=pltpu.PrefetchScalarGridSpec(
            num_scalar_prefetch=1,
            grid=(bsz, ng),
            in_specs=[pl.BlockSpec(memory_space=pltpu.SMEM), tok(width), tok(width), tok(width)]
                     + [page_spec(g) for g in range(gp)] * 2,
            out_specs=tok(width),
            scratch_shapes=[pltpu.VMEM((n_heads * tn, width), BF16),
                            pltpu.VMEM((n_heads * tn, width), F32),
                            pltpu.VMEM((n_heads * tn, 1), F32)]),
        out_shape=jax.ShapeDtypeStruct((bsz, tn, width), F32),
        compiler_params=_params(("parallel", "arbitrary")),
        name="sb_sample",
    )(page_table, bias, qkv3, k3, v3, *([cache_kt] * gp), *([cache_vt] * gp))


def _out_proj_kernel(x_ref, ya_ref, yb_ref, yc_ref, gb_ref, w_ref, g_ref, b_ref, o_ref, *, alpha):
    yb = _rms_norm(yb_ref[...], gb_ref[...])
    y = jnp.concatenate([ya_ref[...], yb, yc_ref[...]], axis=-1).astype(BF16)
    m = jnp.dot(y, w_ref[...], preferred_element_type=F32)
    o_ref[...] = _layer_norm(alpha * x_ref[...] + m, g_ref[...], b_ref[...])


def _out_proj(x2d, ya, yb, yc, g_b, w_bf16, ln_g, ln_b, alpha, tm):
    m, d = x2d.shape
    row = lambda n: pl.BlockSpec((tm, n), lambda i: (i, 0))
    const = lambda shp: pl.BlockSpec(shp, lambda i: (0, 0))
    return pl.pallas_call(
        functools.partial(_out_proj_kernel, alpha=alpha),
        grid=(m // tm,),
        in_specs=[row(d), row(ya.shape[1]), row(yb.shape[1]), row(yc.shape[1]),
                  const((1, yb.shape[1])), const(w_bf16.shape), const((1, d)), const((1, d))],
        out_specs=row(d),
        out_shape=jax.ShapeDtypeStruct((m, d), F32),
        compiler_params=_params(("parallel",)),
        name="out_proj",
    )(x2d, ya, yb, yc, g_b, w_bf16, ln_g, ln_b)


def _mlp_kernel(x_ref, wu_ref, wd_ref, g_ref, b_ref, o_ref, acc_ref, *, alpha):
    f = pl.program_id(1)

    @pl.when(f == 0)
    def _():
        acc_ref[...] = jnp.zeros_like(acc_ref)

    h = jnp.dot(x_ref[...].astype(BF16), wu_ref[...], preferred_element_type=F32)
    h = jnp.square(jnp.maximum(h, 0.0)).astype(BF16)
    acc_ref[...] += jnp.dot(h, wd_ref[...], preferred_element_type=F32)

    @pl.when(f == pl.num_programs(1) - 1)
    def _():
        o_ref[...] = _layer_norm(alpha * x_ref[...] + acc_ref[...], g_ref[...], b_ref[...])


def _mlp(x2d, wu_bf16, wd_bf16, ln_g, ln_b, alpha, tm, tf):
    m, d = x2d.shape
    dff = wu_bf16.shape[1]
    return pl.pallas_call(
        functools.partial(_mlp_kernel, alpha=alpha),
        grid=(m // tm, dff // tf),
        in_specs=[pl.BlockSpec((tm, d), lambda i, f: (i, 0)),
                  pl.BlockSpec((d, tf), lambda i, f: (0, f)),
                  pl.BlockSpec((tf, d), lambda i, f: (f, 0)),
                  pl.BlockSpec((1, d), lambda i, f: (0, 0)),
                  pl.BlockSpec((1, d), lambda i, f: (0, 0))],
        out_specs=pl.BlockSpec((tm, d), lambda i, f: (i, 0)),
        out_shape=jax.ShapeDtypeStruct((m, d), F32),
        scratch_shapes=[pltpu.VMEM((tm, d), F32)],
        compiler_params=_params(("parallel", "arbitrary")),
        name="mlp",
    )(x2d, wu_bf16, wd_bf16, ln_g, ln_b)


def _block_diag(w):
    h, i, j = w.shape
    eye = jnp.eye(h, dtype=w.dtype)
    return jnp.einsum("hij,hg->higj", w, eye).reshape(h * i, h * j)


def _layer(x, lru_buf, lru_h, ccm_buf, past, lw, dims, tiles):
    bsz, t, d = x.shape
    lru_w, sb_w, ccm_w, alpha = dims
    tm, tc, tq, tk, gp = tiles
    m = bsz * t
    heads = sb_w // SB_HEAD_DIM
    x2d = x.reshape(m, d)
    if past is None:
        pa, qkv, kt, vt, pc = _in_proj(x, lw["w_in"], lru_w, sb_w, ccm_w, tm, True)
        k = jnp.transpose(kt.reshape(bsz, heads, SB_HEAD_DIM, t), (0, 3, 1, 2))
        v = jnp.transpose(vt.reshape(bsz, heads, SB_HEAD_DIM, t), (0, 3, 1, 2))
    else:
        pa, qkv, k, v, pc = _in_proj(x2d.reshape(1, m, d), lw["w_in"], lru_w, sb_w, ccm_w, tm, False)
        k = k.reshape(bsz, t, heads, SB_HEAD_DIM)
        v = v.reshape(bsz, t, heads, SB_HEAD_DIM)

    ya, lru_buf_new, lru_h_new = _lru(
        pa.reshape(bsz, t, 2 * lru_w), lru_buf, lru_h.reshape(bsz, 1, lru_w),
        lw["lru_conv_w"], lw["lru_conv_b"], lw["lru_wg"], lw["lru_bg"], lw["lru_lam"], lw["g_a"], tc)
    yc, ccm_buf_new = _ccm(
        pc.reshape(bsz, t, 2 * ccm_w), ccm_buf, lw["ccm_conv_w"], lw["ccm_conv_b"],
        lw["ccm_ln_g"], lw["ccm_ln_b"], lw["g_c"], tc)

    qkv3 = qkv.reshape(bsz, t, 3 * sb_w)
    if past is None:
        yb = _sb_prompt(qkv3, lw["sb_bias"], sb_w, tq, tk)
    else:
        cache_kt, cache_vt, page_table, layer = past
        yb = _sb_sample(qkv3, k.reshape(bsz, t, sb_w), v.reshape(bsz, t, sb_w),
                        cache_kt, cache_vt, page_table, lw["sb_bias"], layer, gp)

    x1 = _out_proj(x2d, ya.reshape(m, lru_w), yb.reshape(m, sb_w), yc.reshape(m, ccm_w),
                   lw["g_b"], lw["w_out"], lw["ln1_g"], lw["ln1_b"], alpha, tm)
    x2 = _mlp(x1, lw["w_up"], lw["w_down"], lw["ln2_g"], lw["ln2_b"], alpha, tm,
              min(1024, lw["w_up"].shape[1]))
    return x2.reshape(bsz, t, d), (k, v, lru_buf_new, lru_h_new.reshape(bsz, lru_w), ccm_buf_new)


def kernel(x_prompt, x_sample, cache_k, cache_v, page_table, state_lru_conv, state_lru_h, state_ccm_conv,
           w_in, sb_bias, lru_conv_w, lru_conv_b, lru_wa, lru_ba, lru_wx, lru_bx, lru_lam,
           ccm_conv_w, ccm_conv_b, ccm_ln_g, ccm_ln_b, g_mix, w_out,
           ln1_g, ln1_b, w_up, w_down, ln2_g, ln2_b):
    depth = w_in.shape[0]
    bsz, seq, d = x_prompt.shape
    dec_b, dec_t, _ = x_sample.shape
    lru_w = lru_lam.shape[1]
    ccm_w = ccm_ln_g.shape[1]
    sb_w = g_mix.shape[1] - lru_w - ccm_w
    alpha = (2 * depth) ** 0.25
    dims = (lru_w, sb_w, ccm_w, alpha)
    n_pool, page = cache_k.shape[1], cache_k.shape[2]
    cache_kt = jnp.transpose(cache_k, (0, 1, 3, 4, 2)).reshape(depth, n_pool, sb_w, page)
    cache_vt = jnp.transpose(cache_v, (0, 1, 3, 4, 2)).reshape(depth, n_pool, sb_w, page)

    zero_lru_buf = jnp.zeros((bsz, LRU_CONV - 1, lru_w), F32)
    zero_lru_h = jnp.zeros((bsz, lru_w), F32)
    zero_ccm_buf = jnp.zeros((bsz, CCM_CONV - 1, ccm_w), F32)

    prompt_tiles = (min(512, bsz * seq), min(512, seq), 128, 256, 0)
    sample_tiles = (dec_b * dec_t, dec_t, 0, 0, min(16, page_table.shape[1]))

    xp, xs = x_prompt, x_sample
    outs_p, outs_s = [], []
    for l in range(depth):
        row = lambda a: a[l].reshape(1, -1)
        lw = {
            "w_in": w_in[l].astype(BF16), "sb_bias": sb_bias[l],
            "lru_conv_w": lru_conv_w[l], "lru_conv_b": row(lru_conv_b),
            "lru_wg": jnp.concatenate([_block_diag(lru_wa[l]), _block_diag(lru_wx[l])], axis=1).astype(BF16),
            "lru_bg": jnp.concatenate([lru_ba[l], lru_bx[l]]).reshape(1, -1),
            "lru_lam": row(lru_lam),
            "ccm_conv_w": ccm_conv_w[l], "ccm_conv_b": row(ccm_conv_b),
            "ccm_ln_g": row(ccm_ln_g), "ccm_ln_b": row(ccm_ln_b),
            "g_a": g_mix[l, :lru_w].reshape(1, -1),
            "g_b": g_mix[l, lru_w:lru_w + sb_w].reshape(1, -1),
            "g_c": g_mix[l, lru_w + sb_w:].reshape(1, -1),
            "w_out": w_out[l].astype(BF16),
            "ln1_g": row(ln1_g), "ln1_b": row(ln1_b),
            "w_up": w_up[l].astype(BF16), "w_down": w_down[l].astype(BF16),
            "ln2_g": row(ln2_g), "ln2_b": row(ln2_b),
        }
        xp, st = _layer(xp, zero_lru_buf, zero_lru_h, zero_ccm_buf, None, lw, dims, prompt_tiles)
        outs_p.append(st)
        xs, st = _layer(xs, state_lru_conv[l], state_lru_h[l], state_ccm_conv[l],
                        (cache_kt, cache_vt, page_table, l), lw, dims, sample_tiles)
        outs_s.append(st)

    stack = lambda outs, n: jnp.stack([o[n] for o in outs])
    return (xp, xs,
            stack(outs_p, 0), stack(outs_p, 1), stack(outs_p, 2), stack(outs_p, 3), stack(outs_p, 4),
            stack(outs_s, 0), stack(outs_s, 1), stack(outs_s, 2), stack(outs_s, 3), stack(outs_s, 4))
```

```python
import functools

import jax
import jax.numpy as jnp
from jax import lax
from jax.experimental import pallas as pl
from jax.experimental.pallas import tpu as pltpu

F32 = jnp.float32
BF16 = jnp.bfloat16

LRU_HEADS = 4
LRU_CONV = 4
LRU_C = 8.0
SB_HEAD_DIM = 64
CCM_CONV = 31
NORM_EPS = 1e-5
PAGE_SIZE = 128
LANES = 128
SUBLANES = 8
VMEM_LIMIT = 48 * 1024 * 1024
LOG2E = 1.4426950408889634
SOFTPLUS2_LINEAR_ABOVE = 64.0
MASKED_LOGIT = -1e30


def _params(sem):
    return pltpu.CompilerParams(dimension_semantics=sem, vmem_limit_bytes=VMEM_LIMIT)


def _sigmoid(x):
    return 1.0 / (1.0 + jnp.exp(-x))


def _softplus(x):
    return jnp.maximum(x, 0.0) + jnp.log1p(jnp.exp(-jnp.abs(x)))


def _sb_mass(mm, bias, tri, valid):
    u = mm + bias
    sp = jnp.where(u > SOFTPLUS2_LINEAR_ABOVE, u, jnp.log(1.0 + jnp.exp2(u)) * LOG2E)
    if valid is not None:
        sp = jnp.where(valid, sp, 0.0)
        u = jnp.where(valid, u, MASKED_LOGIT)
    return u, jnp.dot(sp.astype(BF16), tri, preferred_element_type=F32)


def _sb_weights(u, c, r):
    return jnp.exp2(u - (c + r)).astype(BF16)


def _layer_norm(x, g, b):
    mu = jnp.mean(x, axis=-1, keepdims=True)
    xc = x - mu
    var = jnp.mean(xc * xc, axis=-1, keepdims=True)
    return xc * lax.rsqrt(var + NORM_EPS) * g + b


def _rms_norm(x, g):
    return x * lax.rsqrt(jnp.mean(x * x, axis=-1, keepdims=True) + NORM_EPS) * g


def _in_proj_kernel(x_ref, w_ref, *rest, lru_w, sb_w, kv_transposed):
    pa_ref, qkv_ref, k_ref, v_ref, pc_ref = rest[2:]
    p = jnp.dot(x_ref[...].astype(BF16), w_ref[...], preferred_element_type=F32)
    a_end = 2 * lru_w
    q_end = a_end + sb_w
    k_end = q_end + sb_w
    v_end = k_end + sb_w
    pa_ref[...] = p[:, :a_end]
    q = p[:, a_end:q_end] * (SB_HEAD_DIM ** -0.5 * LOG2E)
    k = p[:, q_end:k_end]
    v = p[:, k_end:v_end]
    qkv_ref[:, :sb_w] = q.astype(BF16)
    qkv_ref[:, sb_w:2 * sb_w] = k.astype(BF16)
    qkv_ref[:, 2 * sb_w:] = v.astype(BF16)
    k_ref[...] = k.T if kv_transposed else k
    v_ref[...] = v.T if kv_transposed else v
    pc_ref[...] = p[:, v_end:]


def _in_proj(x3, w_bf16, lru_w, sb_w, ccm_w, tm, kv_transposed, layer, depth, kv_stack):
    g, r, d = x3.shape
    pw = w_bf16.shape[1]
    row = lambda n: pl.BlockSpec((None, tm, n), lambda b, i: (b, i, 0))
    if kv_transposed:
        kv_spec = pl.BlockSpec((None, None, sb_w, tm), lambda b, i: (layer, b, 0, i))
        assert kv_stack[0].shape == (depth, g, sb_w, r)
    else:
        kv_spec = pl.BlockSpec((None, None, tm, sb_w), lambda b, i: (layer, b, i, 0))
        assert kv_stack[0].shape == (depth, g, r, sb_w)
    kv_shape = jax.ShapeDtypeStruct(kv_stack[0].shape, F32)
    return pl.pallas_call(
        functools.partial(_in_proj_kernel, lru_w=lru_w, sb_w=sb_w, kv_transposed=kv_transposed),
        grid=(g, r // tm),
        in_specs=[row(d), pl.BlockSpec((d, pw), lambda b, i: (0, 0)),
                  pl.BlockSpec(memory_space=pl.ANY), pl.BlockSpec(memory_space=pl.ANY)],
        out_specs=[row(2 * lru_w), row(3 * sb_w), kv_spec, kv_spec, row(2 * ccm_w)],
        out_shape=[jax.ShapeDtypeStruct((g, r, 2 * lru_w), F32),
                   jax.ShapeDtypeStruct((g, r, 3 * sb_w), BF16),
                   kv_shape, kv_shape,
                   jax.ShapeDtypeStruct((g, r, 2 * ccm_w), F32)],
        input_output_aliases={2: 2, 3: 3},
        compiler_params=_params(("parallel", "parallel")),
        name="in_proj",
    )(x3, w_bf16, *kv_stack)


def _shift_rows(x, s, fill, row):
    return jnp.where(row >= s, pltpu.roll(x, s, axis=0), fill)


def _lru_kernel(pa_ref, buf0_ref, h0_ref, cw_ref, cb_ref, wg_ref, bg_ref, lam_ref, g_ref,
                ya_ref, bufn_ref, hn_ref, xbuf, hcar, *, tc, width):
    t = pl.program_id(1)
    keep = LRU_CONV - 1
    base = SUBLANES - keep

    @pl.when(t == 0)
    def _():
        xbuf[base:SUBLANES, :] = buf0_ref[...]
        hcar[...] = h0_ref[...]

    xr = pa_ref[:, :width]
    gr = pa_ref[:, width:]
    xbuf[SUBLANES:SUBLANES + tc, :] = xr
    xc = cb_ref[...] + cw_ref[0:1, :] * xbuf[base:base + tc, :]
    for k in range(1, LRU_CONV):
        xc = xc + cw_ref[k:k + 1, :] * xbuf[base + k:base + k + tc, :]
    tail = xbuf[base + tc:SUBLANES + tc, :]
    xbuf[base:SUBLANES, :] = tail
    bufn_ref[...] = tail

    gates = jnp.dot(xc.astype(BF16), wg_ref[...], preferred_element_type=F32) + bg_ref[...]
    log_a = (-LRU_C) * _sigmoid(gates[:, :width]) * _softplus(-lam_ref[...])
    a = jnp.exp(log_a)
    th = jnp.tanh(log_a)
    b = jnp.sqrt(-2.0 * th / (1.0 - th)) * _sigmoid(gates[:, width:]) * xc

    row = lax.broadcasted_iota(jnp.int32, (tc, width), 0)
    s = 1
    while s < tc:
        b = a * _shift_rows(b, s, 0.0, row) + b
        a = a * _shift_rows(a, s, 1.0, row)
        s *= 2
    h = a * hcar[...] + b
    hcar[...] = h[tc - 1:tc, :]
    hn_ref[...] = h[tc - 1:tc, :]

    gelu = 0.5 * gr * (1.0 + jnp.tanh(0.7978845608028654 * (gr + 0.044715 * gr * gr * gr)))
    ya_ref[...] = _rms_norm(h * gelu, g_ref[...])


def _lru(pa3, buf0, h0, cw, cb, wg, bg, lam, g_a, tc):
    bsz, t, w2 = pa3.shape
    width = w2 // 2
    keep = LRU_CONV - 1
    const = lambda shp: pl.BlockSpec(shp, lambda b, i: (0,) * len(shp))
    return pl.pallas_call(
        functools.partial(_lru_kernel, tc=tc, width=width),
        grid=(bsz, t // tc),
        in_specs=[pl.BlockSpec((None, tc, w2), lambda b, i: (b, i, 0)),
                  pl.BlockSpec((None, keep, width), lambda b, i: (b, 0, 0)),
                  pl.BlockSpec((None, 1, width), lambda b, i: (b, 0, 0)),
                  const((LRU_CONV, width)), const((1, width)), const((width, 2 * width)),
                  const((1, 2 * width)), const((1, width)), const((1, width))],
        out_specs=[pl.BlockSpec((None, tc, width), lambda b, i: (b, i, 0)),
                   pl.BlockSpec((None, keep, width), lambda b, i: (b, 0, 0)),
                   pl.BlockSpec((None, 1, width), lambda b, i: (b, 0, 0))],
        out_shape=[jax.ShapeDtypeStruct((bsz, t, width), F32),
                   jax.ShapeDtypeStruct((bsz, keep, width), F32),
                   jax.ShapeDtypeStruct((bsz, 1, width), F32)],
        scratch_shapes=[pltpu.VMEM((SUBLANES + tc, width), F32), pltpu.VMEM((1, width), F32)],
        compiler_params=_params(("parallel", "arbitrary")),
        name="lru",
    )(pa3, buf0, h0, cw, cb, wg, bg, lam, g_a)


def _ccm_kernel(pc_ref, buf0_ref, cw_ref, cb_ref, lg_ref, lb_ref, g_ref,
                yc_ref, bufn_ref, ubuf, *, tc, width):
    t = pl.program_id(1)
    keep = CCM_CONV - 1
    pad = 4 * SUBLANES
    base = pad - keep

    @pl.when(t == 0)
    def _():
        ubuf[base:pad, :] = buf0_ref[...]

    u = pc_ref[:, :width] * _sigmoid(pc_ref[:, width:])
    ubuf[pad:pad + tc, :] = u
    acc = cb_ref[...] + cw_ref[0:1, :] * ubuf[base:base + tc, :]
    for k in range(1, CCM_CONV):
        acc = acc + cw_ref[k:k + 1, :] * ubuf[base + k:base + k + tc, :]
    tail = ubuf[base + tc:pad + tc, :]
    ubuf[base:pad, :] = tail
    bufn_ref[...] = tail

    y = _layer_norm(acc, lg_ref[...], lb_ref[...])
    y = y * _sigmoid(y)
    yc_ref[...] = _rms_norm(y, g_ref[...])


def _ccm(pc3, buf0, cw, cb, lg, lb, g_c, tc):
    bsz, t, w2 = pc3.shape
    width = w2 // 2
    keep = CCM_CONV - 1
    const = lambda shp: pl.BlockSpec(shp, lambda b, i: (0,) * len(shp))
    return pl.pallas_call(
        functools.partial(_ccm_kernel, tc=tc, width=width),
        grid=(bsz, t // tc),
        in_specs=[pl.BlockSpec((None, tc, w2), lambda b, i: (b, i, 0)),
                  pl.BlockSpec((None, keep, width), lambda b, i: (b, 0, 0)),
                  const((CCM_CONV, width)), const((1, width)), const((1, width)),
                  const((1, width)), const((1, width))],
        out_specs=[pl.BlockSpec((None, tc, width), lambda b, i: (b, i, 0)),
                   pl.BlockSpec((None, keep, width), lambda b, i: (b, 0, 0))],
        out_shape=[jax.ShapeDtypeStruct((bsz, t, width), F32),
                   jax.ShapeDtypeStruct((bsz, keep, width), F32)],
        scratch_shapes=[pltpu.VMEM((4 * SUBLANES + tc, width), F32)],
        compiler_params=_params(("parallel", "arbitrary")),
        name="ccm",
    )(pc3, buf0, cw, cb, lg, lb, g_c)


def _sb_prompt_kernel(bias_ref, q_ref, k_ref, v_ref, o_ref, acc_ref, r_ref, *, tq, tk):
    hp = pl.program_id(1)
    i = pl.program_id(2)
    rows = 2 * tq
    lane = lax.broadcasted_iota(jnp.int32, (tq, LANES), 1)
    qb = q_ref[...]
    zero = jnp.zeros_like(qb)
    q2 = jnp.concatenate([jnp.where(lane < SB_HEAD_DIM, qb, zero),
                          jnp.where(lane >= SB_HEAD_DIM, qb, zero)], axis=0)
    rowi = lax.broadcasted_iota(jnp.int32, (rows, 1), 0)
    bias = jnp.where(rowi < tq, bias_ref[2 * hp], bias_ref[2 * hp + 1]) * LOG2E
    kk = lax.broadcasted_iota(jnp.int32, (tk, tk), 0)
    ss = lax.broadcasted_iota(jnp.int32, (tk, tk), 1)
    tri = jnp.where(kk >= ss, 1.0, 0.0).astype(BF16)

    acc_ref[...] = jnp.zeros_like(acc_ref)
    r_ref[...] = jnp.zeros_like(r_ref)

    def block(j, nsub, masked):
        off = pl.multiple_of(j * tk, tk)
        kb = k_ref[pl.ds(off, nsub * tk), :]
        vb = v_ref[pl.ds(off, nsub * tk), :]
        mm = lax.dot_general(q2, kb, (((1,), (1,)), ((), ())), preferred_element_type=F32)
        valid = None
        if masked:
            qpos = jnp.where(rowi < tq, rowi, rowi - tq)
            valid = lax.broadcasted_iota(jnp.int32, (rows, tk), 1) < qpos
        uc = [_sb_mass(mm[:, s * tk:(s + 1) * tk], bias, tri, valid) for s in range(nsub)]
        r = r_ref[...]
        ws = [None] * nsub
        for s in range(nsub - 1, -1, -1):
            u, c = uc[s]
            ws[s] = _sb_weights(u, c, r)
            r = r + c[:, 0:1]
        r_ref[...] = r
        w = ws[0] if nsub == 1 else jnp.concatenate(ws, axis=1)
        acc_ref[...] += jnp.dot(w, vb, preferred_element_type=F32)

    block(i, 1, True)

    @pl.when(jnp.bitwise_and(i, 1) != 0)
    def _():
        block(i - 1, 1, False)

    @pl.when(jnp.bitwise_and(i, 2) != 0)
    def _():
        block(jnp.bitwise_and(i, -4), 2, False)

    nquads = lax.shift_right_logical(i, 2)

    def body(n, carry):
        block(4 * (nquads - 1 - n), 4, False)
        return carry

    lax.fori_loop(0, nquads, body, 0)
    o_ref[...] = jnp.where(lane < SB_HEAD_DIM, acc_ref[:tq, :], acc_ref[tq:, :])


def _sb_prompt(qkv3, bias, sb_w, tq, tk):
    bsz, t, _ = qkv3.shape
    npair = sb_w // LANES
    assert tq == tk and t % tq == 0
    return pl.pallas_call(
        functools.partial(_sb_prompt_kernel, tq=tq, tk=tk),
        grid_spec=pltpu.PrefetchScalarGridSpec(
            num_scalar_prefetch=0,
            grid=(bsz, npair, t // tq),
            in_specs=[pl.BlockSpec(memory_space=pltpu.SMEM),
                      pl.BlockSpec((None, tq, LANES), lambda b, h, i: (b, i, h)),
                      pl.BlockSpec((None, t, LANES), lambda b, h, i: (b, 0, npair + h)),
                      pl.BlockSpec((None, t, LANES), lambda b, h, i: (b, 0, 2 * npair + h))],
            out_specs=pl.BlockSpec((None, tq, LANES), lambda b, h, i: (b, i, h)),
            scratch_shapes=[pltpu.VMEM((2 * tq, LANES), F32), pltpu.VMEM((2 * tq, 1), F32)]),
        out_shape=jax.ShapeDtypeStruct((bsz, t, sb_w), F32),
        compiler_params=_params(("parallel", "parallel", "arbitrary")),
        name="sb_prompt",
    )(bias, qkv3, qkv3, qkv3)


def _sb_sample_kernel(pt_ref, bias_ref, q_ref, kn_ref, vn_ref, *rest, n_heads, tn, gp):
    del pt_ref
    k_refs = rest[:gp]
    v_refs = rest[gp:2 * gp]
    o_ref, qbd_ref, acc_ref, r_ref = rest[2 * gp:]
    j = pl.program_id(1)
    rows = n_heads * tn
    width = n_heads * SB_HEAD_DIM
    rowi = lax.broadcasted_iota(jnp.int32, (rows, 1), 0)
    row_head = lax.shift_right_logical(rowi, tn.bit_length() - 1)
    row_tok = jnp.bitwise_and(rowi, tn - 1)
    dim_shift = SB_HEAD_DIM.bit_length() - 1
    kk = lax.broadcasted_iota(jnp.int32, (PAGE_SIZE, PAGE_SIZE), 0)
    ss = lax.broadcasted_iota(jnp.int32, (PAGE_SIZE, PAGE_SIZE), 1)
    tri = jnp.where(kk >= ss, 1.0, 0.0).astype(BF16)
    bias = jnp.zeros((rows, 1), F32)
    for h in range(n_heads):
        bias = jnp.where(row_head == h, bias_ref[h] * LOG2E, bias)

    def blocks(kts, vts, valid):
        n = len(kts)
        qbd = qbd_ref[...]
        mm = jnp.concatenate([jnp.dot(qbd, kt, preferred_element_type=F32) for kt in kts], axis=0)
        if valid is not None:
            valid = jnp.concatenate([valid] * n, axis=0)
        u, c = _sb_mass(mm, jnp.concatenate([bias] * n, axis=0), tri, valid)
        r = r_ref[...]
        rs = []
        for p in range(n):
            rs.append(r)
            r = r + c[p * rows:(p + 1) * rows, 0:1]
        r_ref[...] = r
        w = _sb_weights(u, c, jnp.concatenate(rs, axis=0))
        acc = acc_ref[...]
        for p in range(n):
            acc = acc + lax.dot_general(w[p * rows:(p + 1) * rows, :], vts[p], (((1,), (1,)), ((), ())),
                                        preferred_element_type=F32)
        acc_ref[...] = acc

    @pl.when(j == 0)
    def _():
        q = q_ref[...].astype(F32)
        qt = jnp.concatenate([q] * n_heads, axis=0)
        col = lax.broadcasted_iota(jnp.int32, (rows, width), 1)
        qbd_ref[...] = jnp.where(lax.shift_right_logical(col, dim_shift) == row_head,
                                 qt, 0.0).astype(BF16)
        acc_ref[...] = jnp.zeros_like(acc_ref)
        r_ref[...] = jnp.zeros_like(r_ref)
        pad = jnp.zeros((PAGE_SIZE - tn, width), F32)
        kn = jnp.concatenate([kn_ref[...], pad], axis=0).T.astype(BF16)
        vn = jnp.concatenate([vn_ref[...], pad], axis=0).T.astype(BF16)
        kpos = lax.broadcasted_iota(jnp.int32, (rows, PAGE_SIZE), 1)
        blocks([kn], [vn], kpos < row_tok)

    order = range(gp - 1, -1, -1)
    blocks([k_refs[g][...].astype(BF16) for g in order],
           [v_refs[g][...].astype(BF16) for g in order], None)

    @pl.when(j == pl.num_programs(1) - 1)
    def _():
        col = lax.broadcasted_iota(jnp.int32, (tn, width), 1)
        out = jnp.zeros((tn, width), F32)
        for h in range(n_heads):
            out = out + jnp.where(lax.shift_right_logical(col, dim_shift) == h,
                                  acc_ref[h * tn:(h + 1) * tn, :], 0.0)
        o_ref[...] = out


def _sb_sample(qkv3, k3, v3, cache_kt, cache_vt, page_table, bias, layer, gp):
    bsz, tn, w3 = qkv3.shape
    width = w3 // 3
    n_heads = width // SB_HEAD_DIM
    n_pages = page_table.shape[1]
    ng = n_pages // gp

    def page_spec(g):
        return pl.BlockSpec((None, None, width, PAGE_SIZE),
                            lambda b, j, pt, g=g: (layer, pt[b, (ng - 1 - j) * gp + g], 0, 0))

    tok = lambda n: pl.BlockSpec((None, tn, n), lambda b, j, pt: (b, 0, 0))
    return pl.pallas_call(
        functools.partial(_sb_sample_kernel, n_heads=n_heads, tn=tn, gp=gp),
        grid_spec=pltpu.PrefetchScalarGridSpec(
            num_scalar_prefetch=1,
            grid=(bsz, ng),
            in_specs=[pl.BlockSpec(memory_space=pltpu.SMEM), tok(width), tok(width), tok(width)]
                     + [page_spec(g) for g in range(gp)] * 2,
            out_specs=tok(width),
            scratch_shapes=[pltpu.VMEM((n_heads * tn, width), BF16),
                            pltpu.VMEM((n_heads * tn, width), F32),
                            pltpu.VMEM((n_heads * tn, 1), F32)]),
        out_shape=jax.ShapeDtypeStruct((bsz, tn, width), F32),
        compiler_params=_params(("parallel", "arbitrary")),
        name="sb_sample",
    )(page_table, bias, qkv3, k3, v3, *([cache_kt] * gp), *([cache_vt] * gp))


def _out_proj_kernel(x_ref, ya_ref, yb_ref, yc_ref, gb_ref, w_ref, g_ref, b_ref, o_ref, *, alpha):
    yb = _rms_norm(yb_ref[...], gb_ref[...])
    y = jnp.concatenate([ya_ref[...], yb, yc_ref[...]], axis=-1).astype(BF16)
    m = jnp.dot(y, w_ref[...], preferred_element_type=F32)
    o_ref[...] = _layer_norm(alpha * x_ref[...] + m, g_ref[...], b_ref[...])


def _out_proj(x2d, ya, yb, yc, g_b, w_bf16, ln_g, ln_b, alpha, tm):
    m, d = x2d.shape
    row = lambda n: pl.BlockSpec((tm, n), lambda i: (i, 0))
    const = lambda shp: pl.BlockSpec(shp, lambda i: (0, 0))
    return pl.pallas_call(
        functools.partial(_out_proj_kernel, alpha=alpha),
        grid=(m // tm,),
        in_specs=[row(d), row(ya.shape[1]), row(yb.shape[1]), row(yc.shape[1]),
                  const((1, yb.shape[1])), const(w_bf16.shape), const((1, d)), const((1, d))],
        out_specs=row(d),
        out_shape=jax.ShapeDtypeStruct((m, d), F32),
        compiler_params=_params(("parallel",)),
        name="out_proj",
    )(x2d, ya, yb, yc, g_b, w_bf16, ln_g, ln_b)


def _mlp_kernel(x_ref, wu_ref, wd_ref, g_ref, b_ref, o_ref, acc_ref, *, alpha):
    f = pl.program_id(1)

    @pl.when(f == 0)
    def _():
        acc_ref[...] = jnp.zeros_like(acc_ref)

    h = jnp.dot(x_ref[...].astype(BF16), wu_ref[...], preferred_element_type=F32)
    h = jnp.square(jnp.maximum(h, 0.0)).astype(BF16)
    acc_ref[...] += jnp.dot(h, wd_ref[...], preferred_element_type=F32)

    @pl.when(f == pl.num_programs(1) - 1)
    def _():
        o_ref[...] = _layer_norm(alpha * x_ref[...] + acc_ref[...], g_ref[...], b_ref[...])


def _mlp(x2d, wu_bf16, wd_bf16, ln_g, ln_b, alpha, tm, tf):
    m, d = x2d.shape
    dff = wu_bf16.shape[1]
    return pl.pallas_call(
        functools.partial(_mlp_kernel, alpha=alpha),
        grid=(m // tm, dff // tf),
        in_specs=[pl.BlockSpec((tm, d), lambda i, f: (i, 0)),
                  pl.BlockSpec((d, tf), lambda i, f: (0, f)),
                  pl.BlockSpec((tf, d), lambda i, f: (f, 0)),
                  pl.BlockSpec((1, d), lambda i, f: (0, 0)),
                  pl.BlockSpec((1, d), lambda i, f: (0, 0))],
        out_specs=pl.BlockSpec((tm, d), lambda i, f: (i, 0)),
        out_shape=jax.ShapeDtypeStruct((m, d), F32),
        scratch_shapes=[pltpu.VMEM((tm, d), F32)],
        compiler_params=_params(("parallel", "arbitrary")),
        name="mlp",
    )(x2d, wu_bf16, wd_bf16, ln_g, ln_b)


def _block_diag(w):
    h, i, j = w.shape
    eye = jnp.eye(h, dtype=w.dtype)
    return jnp.einsum("hij,hg->higj", w, eye).reshape(h * i, h * j)


def _layer(x, lru_buf, lru_h, ccm_buf, past, kv_stack, layer, depth, lw, dims, tiles):
    bsz, t, d = x.shape
    lru_w, sb_w, ccm_w, alpha = dims
    tm, tc, tq, tk, gp = tiles
    m = bsz * t
    x2d = x.reshape(m, d)
    if past is None:
        pa, qkv, k_stack, v_stack, pc = _in_proj(x, lw["w_in"], lru_w, sb_w, ccm_w, tm, True,
                                                 layer, depth, kv_stack)
    else:
        pa, qkv, k_stack, v_stack, pc = _in_proj(x2d.reshape(1, m, d), lw["w_in"], lru_w, sb_w, ccm_w,
                                                 tm, False, layer, depth, kv_stack)

    ya, lru_buf_new, lru_h_new = _lru(
        pa.reshape(bsz, t, 2 * lru_w), lru_buf, lru_h.reshape(bsz, 1, lru_w),
        lw["lru_conv_w"], lw["lru_conv_b"], lw["lru_wg"], lw["lru_bg"], lw["lru_lam"], lw["g_a"], tc)
    yc, ccm_buf_new = _ccm(
        pc.reshape(bsz, t, 2 * ccm_w), ccm_buf, lw["ccm_conv_w"], lw["ccm_conv_b"],
        lw["ccm_ln_g"], lw["ccm_ln_b"], lw["g_c"], tc)

    qkv3 = qkv.reshape(bsz, t, 3 * sb_w)
    if past is None:
        yb = _sb_prompt(qkv3, lw["sb_bias"], sb_w, tq, tk)
    else:
        cache_kt, cache_vt, page_table = past
        yb = _sb_sample(qkv3, k_stack[layer].reshape(bsz, t, sb_w), v_stack[layer].reshape(bsz, t, sb_w),
                        cache_kt, cache_vt, page_table, lw["sb_bias"], layer, gp)

    x1 = _out_proj(x2d, ya.reshape(m, lru_w), yb.reshape(m, sb_w), yc.reshape(m, ccm_w),
                   lw["g_b"], lw["w_out"], lw["ln1_g"], lw["ln1_b"], alpha, tm)
    x2 = _mlp(x1, lw["w_up"], lw["w_down"], lw["ln2_g"], lw["ln2_b"], alpha, tm,
              min(1024, lw["w_up"].shape[1]))
    return (x2.reshape(bsz, t, d), (k_stack, v_stack),
            (lru_buf_new, lru_h_new.reshape(bsz, lru_w), ccm_buf_new))


def kernel(x_prompt, x_sample, cache_k, cache_v, page_table, state_lru_conv, state_lru_h, state_ccm_conv,
           w_in, sb_bias, lru_conv_w, lru_conv_b, lru_wa, lru_ba, lru_wx, lru_bx, lru_lam,
           ccm_conv_w, ccm_conv_b, ccm_ln_g, ccm_ln_b, g_mix, w_out,
           ln1_g, ln1_b, w_up, w_down, ln2_g, ln2_b):
    depth = w_in.shape[0]
    bsz, seq, d = x_prompt.shape
    dec_b, dec_t, _ = x_sample.shape
    lru_w = lru_lam.shape[1]
    ccm_w = ccm_ln_g.shape[1]
    sb_w = g_mix.shape[1] - lru_w - ccm_w
    alpha = (2 * depth) ** 0.25
    dims = (lru_w, sb_w, ccm_w, alpha)
    n_pool, page = cache_k.shape[1], cache_k.shape[2]
    cache_kt = jnp.transpose(cache_k, (0, 1, 3, 4, 2)).reshape(depth, n_pool, sb_w, page)
    cache_vt = jnp.transpose(cache_v, (0, 1, 3, 4, 2)).reshape(depth, n_pool, sb_w, page)

    zero_lru_buf = jnp.zeros((bsz, LRU_CONV - 1, lru_w), F32)
    zero_lru_h = jnp.zeros((bsz, lru_w), F32)
    zero_ccm_buf = jnp.zeros((bsz, CCM_CONV - 1, ccm_w), F32)

    prompt_tiles = (min(512, bsz * seq), min(512, seq), 256, 256, 0)
    sample_tiles = (dec_b * dec_t, dec_t, 0, 0, min(16, page_table.shape[1]))

    heads = sb_w // SB_HEAD_DIM
    kv_p = tuple(jnp.zeros((depth, bsz, sb_w, seq), F32) for _ in range(2))
    kv_s = tuple(jnp.zeros((depth, 1, dec_b * dec_t, sb_w), F32) for _ in range(2))

    xp, xs = x_prompt, x_sample
    outs_p, outs_s = [], []
    for l in range(depth):
        row = lambda a: a[l].reshape(1, -1)
        lw = {
            "w_in": w_in[l].astype(BF16), "sb_bias": sb_bias[l],
            "lru_conv_w": lru_conv_w[l], "lru_conv_b": row(lru_conv_b),
            "lru_wg": jnp.concatenate([_block_diag(lru_wa[l]), _block_diag(lru_wx[l])], axis=1).astype(BF16),
            "lru_bg": jnp.concatenate([lru_ba[l], lru_bx[l]]).reshape(1, -1),
            "lru_lam": row(lru_lam),
            "ccm_conv_w": ccm_conv_w[l], "ccm_conv_b": row(ccm_conv_b),
            "ccm_ln_g": row(ccm_ln_g), "ccm_ln_b": row(ccm_ln_b),
            "g_a": g_mix[l, :lru_w].reshape(1, -1),
            "g_b": g_mix[l, lru_w:lru_w + sb_w].reshape(1, -1),
            "g_c": g_mix[l, lru_w + sb_w:].reshape(1, -1),
            "w_out": w_out[l].astype(BF16),
            "ln1_g": row(ln1_g), "ln1_b": row(ln1_b),
            "w_up": w_up[l].astype(BF16), "w_down": w_down[l].astype(BF16),
            "ln2_g": row(ln2_g), "ln2_b": row(ln2_b),
        }
        xp, kv_p, st = _layer(xp, zero_lru_buf, zero_lru_h, zero_ccm_buf, None, kv_p, l, depth,
                              lw, dims, prompt_tiles)
        outs_p.append(st)
        xs, kv_s, st = _layer(xs, state_lru_conv[l], state_lru_h[l], state_ccm_conv[l],
                              (cache_kt, cache_vt, page_table), kv_s, l, depth, lw, dims, sample_tiles)
        outs_s.append(st)

    stack = lambda outs, n: jnp.stack([o[n] for o in outs])
    kv_prompt = [jnp.transpose(a.reshape(depth, bsz, heads, SB_HEAD_DIM, seq), (0, 1, 4, 2, 3)) for a in kv_p]
    kv_sample = [a.reshape(depth, dec_b, dec_t, heads, SB_HEAD_DIM) for a in kv_s]
    return (xp, xs, kv_prompt[0], kv_prompt[1],
            stack(outs_p, 0), stack(outs_p, 1), stack(outs_p, 2),
            kv_sample[0], kv_sample[1],
            stack(outs_s, 0), stack(outs_s, 1), stack(outs_s, 2))
```

```python
import functools

import jax
import jax.numpy as jnp
from jax import lax
from jax.experimental import pallas as pl
from jax.experimental.pallas import tpu as pltpu

F32 = jnp.float32
BF16 = jnp.bfloat16

LRU_HEADS = 4
LRU_CONV = 4
LRU_C = 8.0
SB_HEAD_DIM = 64
CCM_CONV = 31
NORM_EPS = 1e-5
PAGE_SIZE = 128
LANES = 128
SUBLANES = 8
VMEM_LIMIT = 48 * 1024 * 1024
LOG2E = 1.4426950408889634
SOFTPLUS2_LINEAR_ABOVE = 64.0
MASKED_LOGIT = -1e30
SB_WIDE = 4


def _params(sem):
    return pltpu.CompilerParams(dimension_semantics=sem, vmem_limit_bytes=VMEM_LIMIT)


def _sigmoid(x):
    return 1.0 / (1.0 + jnp.exp(-x))


def _softplus(x):
    return jnp.maximum(x, 0.0) + jnp.log1p(jnp.exp(-jnp.abs(x)))


def _sb_mass(mm, bias, tri, valid):
    u = mm + bias
    sp = jnp.where(u > SOFTPLUS2_LINEAR_ABOVE, u, jnp.log(1.0 + jnp.exp2(u)) * LOG2E)
    if valid is not None:
        sp = jnp.where(valid, sp, 0.0)
        u = jnp.where(valid, u, MASKED_LOGIT)
    return u, jnp.dot(sp.astype(BF16), tri, preferred_element_type=F32)


def _sb_weights(u, c, r):
    return jnp.exp2(u - (c + r)).astype(BF16)


def _layer_norm(x, g, b):
    mu = jnp.mean(x, axis=-1, keepdims=True)
    xc = x - mu
    var = jnp.mean(xc * xc, axis=-1, keepdims=True)
    return xc * lax.rsqrt(var + NORM_EPS) * g + b


def _rms_norm(x, g):
    return x * lax.rsqrt(jnp.mean(x * x, axis=-1, keepdims=True) + NORM_EPS) * g


def _in_proj_kernel(x_ref, w_ref, *rest, lru_w, sb_w, kv_transposed):
    pa_ref, qkv_ref, k_ref, v_ref, pc_ref = rest[2:]
    p = jnp.dot(x_ref[...].astype(BF16), w_ref[...], preferred_element_type=F32)
    a_end = 2 * lru_w
    q_end = a_end + sb_w
    k_end = q_end + sb_w
    v_end = k_end + sb_w
    pa_ref[...] = p[:, :a_end]
    q = p[:, a_end:q_end] * (SB_HEAD_DIM ** -0.5 * LOG2E)
    k = p[:, q_end:k_end]
    v = p[:, k_end:v_end]
    qkv_ref[:, :sb_w] = q.astype(BF16)
    qkv_ref[:, sb_w:2 * sb_w] = k.astype(BF16)
    qkv_ref[:, 2 * sb_w:] = v.astype(BF16)
    k_ref[...] = k.T if kv_transposed else k
    v_ref[...] = v.T if kv_transposed else v
    pc_ref[...] = p[:, v_end:]


def _in_proj(x3, w_bf16, lru_w, sb_w, ccm_w, tm, kv_transposed, layer, depth, kv_stack):
    g, r, d = x3.shape
    pw = w_bf16.shape[1]
    row = lambda n: pl.BlockSpec((None, tm, n), lambda b, i: (b, i, 0))
    if kv_transposed:
        kv_spec = pl.BlockSpec((None, None, sb_w, tm), lambda b, i: (layer, b, 0, i))
        assert kv_stack[0].shape == (depth, g, sb_w, r)
    else:
        kv_spec = pl.BlockSpec((None, None, tm, sb_w), lambda b, i: (layer, b, i, 0))
        assert kv_stack[0].shape == (depth, g, r, sb_w)
    kv_shape = jax.ShapeDtypeStruct(kv_stack[0].shape, F32)
    return pl.pallas_call(
        functools.partial(_in_proj_kernel, lru_w=lru_w, sb_w=sb_w, kv_transposed=kv_transposed),
        grid=(g, r // tm),
        in_specs=[row(d), pl.BlockSpec((d, pw), lambda b, i: (0, 0)),
                  pl.BlockSpec(memory_space=pl.ANY), pl.BlockSpec(memory_space=pl.ANY)],
        out_specs=[row(2 * lru_w), row(3 * sb_w), kv_spec, kv_spec, row(2 * ccm_w)],
        out_shape=[jax.ShapeDtypeStruct((g, r, 2 * lru_w), F32),
                   jax.ShapeDtypeStruct((g, r, 3 * sb_w), BF16),
                   kv_shape, kv_shape,
                   jax.ShapeDtypeStruct((g, r, 2 * ccm_w), F32)],
        input_output_aliases={2: 2, 3: 3},
        compiler_params=_params(("parallel", "parallel")),
        name="in_proj",
    )(x3, w_bf16, *kv_stack)


def _shift_rows(x, s, fill, row):
    if s % SUBLANES == 0:
        return jnp.concatenate([jnp.full((s, x.shape[1]), fill, x.dtype), x[:x.shape[0] - s, :]], axis=0)
    return jnp.where(row >= s, pltpu.roll(x, s, axis=0), fill)


def _lru_kernel(pa_ref, buf0_ref, h0_ref, cw_ref, cb_ref, wg_ref, bg_ref, lam_ref, g_ref,
                ya_ref, bufn_ref, hn_ref, xbuf, hcar, decay, *, tc, width):
    t = pl.program_id(1)
    keep = LRU_CONV - 1
    base = SUBLANES - keep

    @pl.when(t == 0)
    def _():
        xbuf[base:SUBLANES, :] = buf0_ref[...]
        hcar[...] = h0_ref[...]
        decay[...] = (-LRU_C) * _softplus(-lam_ref[...])

    xr = pa_ref[:, :width]
    gr = pa_ref[:, width:]
    xbuf[SUBLANES:SUBLANES + tc, :] = xr
    xc = cb_ref[...] + cw_ref[0:1, :] * xbuf[base:base + tc, :]
    for k in range(1, LRU_CONV):
        xc = xc + cw_ref[k:k + 1, :] * xbuf[base + k:base + k + tc, :]
    tail = xbuf[base + tc:SUBLANES + tc, :]
    xbuf[base:SUBLANES, :] = tail
    bufn_ref[...] = tail

    gates = jnp.dot(xc.astype(BF16), wg_ref[...], preferred_element_type=F32) + bg_ref[...]
    log_a = _sigmoid(gates[:, :width]) * decay[...]
    a = jnp.exp(log_a)
    th = jnp.tanh(log_a)
    em = 2.0 * th / (th - 1.0)
    root = jnp.where(em > 0.0, em * lax.rsqrt(em), 0.0)
    b = root * _sigmoid(gates[:, width:]) * xc

    row = lax.broadcasted_iota(jnp.int32, (tc, width), 0)
    s = 1
    while s < tc:
        b = a * _shift_rows(b, s, 0.0, row) + b
        a = a * _shift_rows(a, s, 1.0, row)
        s *= 2
    h = a * hcar[...] + b
    hcar[...] = h[tc - 1:tc, :]
    hn_ref[...] = h[tc - 1:tc, :]

    gelu = 0.5 * gr * (1.0 + jnp.tanh(0.7978845608028654 * (gr + 0.044715 * gr * gr * gr)))
    ya_ref[...] = _rms_norm(h * gelu, g_ref[...])


def _lru(pa3, buf0, h0, cw, cb, wg, bg, lam, g_a, tc):
    bsz, t, w2 = pa3.shape
    width = w2 // 2
    keep = LRU_CONV - 1
    const = lambda shp: pl.BlockSpec(shp, lambda b, i: (0,) * len(shp))
    return pl.pallas_call(
        functools.partial(_lru_kernel, tc=tc, width=width),
        grid=(bsz, t // tc),
        in_specs=[pl.BlockSpec((None, tc, w2), lambda b, i: (b, i, 0)),
                  pl.BlockSpec((None, keep, width), lambda b, i: (b, 0, 0)),
                  pl.BlockSpec((None, 1, width), lambda b, i: (b, 0, 0)),
                  const((LRU_CONV, width)), const((1, width)), const((width, 2 * width)),
                  const((1, 2 * width)), const((1, width)), const((1, width))],
        out_specs=[pl.BlockSpec((None, tc, width), lambda b, i: (b, i, 0)),
                   pl.BlockSpec((None, keep, width), lambda b, i: (b, 0, 0)),
                   pl.BlockSpec((None, 1, width), lambda b, i: (b, 0, 0))],
        out_shape=[jax.ShapeDtypeStruct((bsz, t, width), F32),
                   jax.ShapeDtypeStruct((bsz, keep, width), F32),
                   jax.ShapeDtypeStruct((bsz, 1, width), F32)],
        scratch_shapes=[pltpu.VMEM((SUBLANES + tc, width), F32), pltpu.VMEM((1, width), F32),
                        pltpu.VMEM((1, width), F32)],
        compiler_params=_params(("parallel", "arbitrary")),
        name="lru",
    )(pa3, buf0, h0, cw, cb, wg, bg, lam, g_a)


def _ccm_kernel(pc_ref, buf0_ref, cw_ref, cb_ref, lg_ref, lb_ref, g_ref,
                yc_ref, bufn_ref, ubuf, *, tc, width):
    t = pl.program_id(1)
    keep = CCM_CONV - 1
    pad = 4 * SUBLANES
    base = pad - keep

    @pl.when(t == 0)
    def _():
        ubuf[0:SUBLANES, :] = jnp.zeros((SUBLANES, width), F32)
        ubuf[base:pad, :] = buf0_ref[...]

    u = pc_ref[:, :width] * _sigmoid(pc_ref[:, width:])
    ubuf[pad:pad + tc, :] = u
    acc = cb_ref[...]
    for res in range(SUBLANES):
        ext = 0 if res == 0 else SUBLANES
        part = None
        for q in range(pad // SUBLANES + 1):
            k = q * SUBLANES + res - base
            if 0 <= k < CCM_CONV:
                term = cw_ref[k:k + 1, :] * ubuf[q * SUBLANES:q * SUBLANES + tc + ext, :]
                part = term if part is None else part + term
        acc = acc + (part if res == 0 else part[res:res + tc, :])
    tail = ubuf[base + tc:pad + tc, :]
    ubuf[base:pad, :] = tail
    bufn_ref[...] = tail

    y = _layer_norm(acc, lg_ref[...], lb_ref[...])
    y = y * _sigmoid(y)
    yc_ref[...] = _rms_norm(y, g_ref[...])


def _ccm(pc3, buf0, cw, cb, lg, lb, g_c, tc):
    bsz, t, w2 = pc3.shape
    width = w2 // 2
    keep = CCM_CONV - 1
    const = lambda shp: pl.BlockSpec(shp, lambda b, i: (0,) * len(shp))
    return pl.pallas_call(
        functools.partial(_ccm_kernel, tc=tc, width=width),
        grid=(bsz, t // tc),
        in_specs=[pl.BlockSpec((None, tc, w2), lambda b, i: (b, i, 0)),
                  pl.BlockSpec((None, keep, width), lambda b, i: (b, 0, 0)),
                  const((CCM_CONV, width)), const((1, width)), const((1, width)),
                  const((1, width)), const((1, width))],
        out_specs=[pl.BlockSpec((None, tc, width), lambda b, i: (b, i, 0)),
                   pl.BlockSpec((None, keep, width), lambda b, i: (b, 0, 0))],
        out_shape=[jax.ShapeDtypeStruct((bsz, t, width), F32),
                   jax.ShapeDtypeStruct((bsz, keep, width), F32)],
        scratch_shapes=[pltpu.VMEM((4 * SUBLANES + tc, width), F32)],
        compiler_params=_params(("parallel", "arbitrary")),
        name="ccm",
    )(pc3, buf0, cw, cb, lg, lb, g_c)


def _sb_prompt_kernel(bias_ref, q_ref, k_ref, v_ref, o_ref, acc_ref, r_ref, *, tq, tk):
    hp = pl.program_id(1)
    rows = 2 * tq
    lane = lax.broadcasted_iota(jnp.int32, (tq, LANES), 1)
    rowi = lax.broadcasted_iota(jnp.int32, (rows, 1), 0)
    bias = jnp.where(rowi < tq, bias_ref[2 * hp], bias_ref[2 * hp + 1]) * LOG2E
    kk = lax.broadcasted_iota(jnp.int32, (tk, tk), 0)
    ss = lax.broadcasted_iota(jnp.int32, (tk, tk), 1)
    tri = jnp.where(kk >= ss, 1.0, 0.0).astype(BF16)

    def block(q2, j, nsub, first_is_diagonal):
        offs = [pl.multiple_of((j + s) * tk, tk) for s in range(nsub - 1, -1, -1)]
        kb = jnp.concatenate([k_ref[pl.ds(o, tk), :] for o in offs], axis=0)
        vb = jnp.concatenate([v_ref[pl.ds(o, tk), :] for o in offs], axis=0)
        mm = lax.dot_general(q2, kb, (((1,), (1,)), ((), ())), preferred_element_type=F32)
        r = r_ref[...]
        ws = []
        for p in range(nsub):
            valid = None
            if first_is_diagonal and p == 0:
                qpos = jnp.where(rowi < tq, rowi, rowi - tq)
                valid = lax.broadcasted_iota(jnp.int32, (rows, tk), 1) < qpos
            u, c = _sb_mass(mm[:, p * tk:(p + 1) * tk], bias, tri, valid)
            ws.append(_sb_weights(u, c, r))
            r = r + c[:, 0:1]
        r_ref[...] = r
        w = ws[0] if nsub == 1 else jnp.concatenate(ws, axis=1)
        acc_ref[...] += jnp.dot(w, vb, preferred_element_type=F32)

    def query_block(i, carry):
        qoff = pl.multiple_of(i * tq, tq)
        qb = q_ref[pl.ds(qoff, tq), :]
        zero = jnp.zeros_like(qb)
        q2 = jnp.concatenate([jnp.where(lane < SB_HEAD_DIM, qb, zero),
                              jnp.where(lane >= SB_HEAD_DIM, qb, zero)], axis=0)
        acc_ref[...] = jnp.zeros_like(acc_ref)
        r_ref[...] = jnp.zeros_like(r_ref)

        n_lead = jnp.bitwise_and(i, SB_WIDE - 1)
        for lead in range(SB_WIDE):
            @pl.when(n_lead == lead)
            def _(lead=lead):
                block(q2, i - lead, lead + 1, True)

        n_wide = lax.shift_right_logical(i, SB_WIDE.bit_length() - 1)

        def body(n, c):
            block(q2, SB_WIDE * (n_wide - 1 - n), SB_WIDE, False)
            return c

        lax.fori_loop(0, n_wide, body, 0)
        o_ref[pl.ds(qoff, tq), :] = jnp.where(lane < SB_HEAD_DIM, acc_ref[:tq, :], acc_ref[tq:, :])
        return carry

    lax.fori_loop(0, q_ref.shape[0] // tq, query_block, 0)


def _sb_prompt(qkv3, bias, sb_w, tq, tk):
    bsz, t, _ = qkv3.shape
    npair = sb_w // LANES
    assert tq == tk and t % tq == 0
    return pl.pallas_call(
        functools.partial(_sb_prompt_kernel, tq=tq, tk=tk),
        grid_spec=pltpu.PrefetchScalarGridSpec(
            num_scalar_prefetch=0,
            grid=(bsz, npair),
            in_specs=[pl.BlockSpec(memory_space=pltpu.SMEM),
                      pl.BlockSpec((None, t, LANES), lambda b, h: (b, 0, h)),
                      pl.BlockSpec((None, t, LANES), lambda b, h: (b, 0, npair + h)),
                      pl.BlockSpec((None, t, LANES), lambda b, h: (b, 0, 2 * npair + h))],
            out_specs=pl.BlockSpec((None, t, LANES), lambda b, h: (b, 0, h)),
            scratch_shapes=[pltpu.VMEM((2 * tq, LANES), F32), pltpu.VMEM((2 * tq, 1), F32)]),
        out_shape=jax.ShapeDtypeStruct((bsz, t, sb_w), F32),
        compiler_params=_params(("parallel", "parallel")),
        name="sb_prompt",
    )(bias, qkv3, qkv3, qkv3)


def _sb_sample_kernel(pt_ref, bias_ref, q_ref, kn_ref, vn_ref, *rest, n_heads, tn, gp):
    del pt_ref
    k_refs = rest[:gp]
    v_refs = rest[gp:2 * gp]
    o_ref, qbd_ref, acc_ref, r_ref = rest[2 * gp:]
    j = pl.program_id(1)
    rows = n_heads * tn
    width = n_heads * SB_HEAD_DIM
    rowi = lax.broadcasted_iota(jnp.int32, (rows, 1), 0)
    row_head = lax.shift_right_logical(rowi, tn.bit_length() - 1)
    row_tok = jnp.bitwise_and(rowi, tn - 1)
    dim_shift = SB_HEAD_DIM.bit_length() - 1
    kk = lax.broadcasted_iota(jnp.int32, (PAGE_SIZE, PAGE_SIZE), 0)
    ss = lax.broadcasted_iota(jnp.int32, (PAGE_SIZE, PAGE_SIZE), 1)
    tri = jnp.where(kk >= ss, 1.0, 0.0).astype(BF16)
    bias = jnp.zeros((rows, 1), F32)
    for h in range(n_heads):
        bias = jnp.where(row_head == h, bias_ref[h] * LOG2E, bias)

    def blocks(kts, vts, valid):
        n = len(kts)
        qbd = qbd_ref[...]
        mm = jnp.concatenate([jnp.dot(qbd, kt, preferred_element_type=F32) for kt in kts], axis=0)
        if valid is not None:
            valid = jnp.concatenate([valid] * n, axis=0)
        u, c = _sb_mass(mm, jnp.concatenate([bias] * n, axis=0), tri, valid)
        r = r_ref[...]
        rs = []
        for p in range(n):
            rs.append(r)
            r = r + c[p * rows:(p + 1) * rows, 0:1]
        r_ref[...] = r
        w = _sb_weights(u, c, jnp.concatenate(rs, axis=0))
        acc = acc_ref[...]
        for p in range(n):
            acc = acc + lax.dot_general(w[p * rows:(p + 1) * rows, :], vts[p], (((1,), (1,)), ((), ())),
                                        preferred_element_type=F32)
        acc_ref[...] = acc

    @pl.when(j == 0)
    def _():
        q = q_ref[...].astype(F32)
        qt = jnp.concatenate([q] * n_heads, axis=0)
        col = lax.broadcasted_iota(jnp.int32, (rows, width), 1)
        qbd_ref[...] = jnp.where(lax.shift_right_logical(col, dim_shift) == row_head,
                                 qt, 0.0).astype(BF16)
        acc_ref[...] = jnp.zeros_like(acc_ref)
        r_ref[...] = jnp.zeros_like(r_ref)
        pad = jnp.zeros((PAGE_SIZE - tn, width), F32)
        kn = jnp.concatenate([kn_ref[...], pad], axis=0).T.astype(BF16)
        vn = jnp.concatenate([vn_ref[...], pad], axis=0).T.astype(BF16)
        kpos = lax.broadcasted_iota(jnp.int32, (rows, PAGE_SIZE), 1)
        blocks([kn], [vn], kpos < row_tok)

    order = range(gp - 1, -1, -1)
    blocks([k_refs[g][...].astype(BF16) for g in order],
           [v_refs[g][...].astype(BF16) for g in order], None)

    @pl.when(j == pl.num_programs(1) - 1)
    def _():
        col = lax.broadcasted_iota(jnp.int32, (tn, width), 1)
        out = jnp.zeros((tn, width), F32)
        for h in range(n_heads):
            out = out + jnp.where(lax.shift_right_logical(col, dim_shift) == h,
                                  acc_ref[h * tn:(h + 1) * tn, :], 0.0)
        o_ref[...] = out


def _sb_sample(qkv3, k3, v3, cache_kt, cache_vt, page_table, bias, layer, gp):
    bsz, tn, w3 = qkv3.shape
    width = w3 // 3
    n_heads = width // SB_HEAD_DIM
    n_pages = page_table.shape[1]
    ng = n_pages // gp

    def page_spec(g):
        return pl.BlockSpec((None, None, width, PAGE_SIZE),
                            lambda b, j, pt, g=g: (layer, pt[b, (ng - 1 - j) * gp + g], 0, 0))

    tok = lambda n: pl.BlockSpec((None, tn, n), lambda b, j, pt: (b, 0, 0))
    return pl.pallas_call(
        functools.partial(_sb_sample_kernel, n_heads=n_heads, tn=tn, gp=gp),
        grid_spec=pltpu.PrefetchScalarGridSpec(
            num_scalar_prefetch=1,
            grid=(bsz, ng),
            in_specs=[pl.BlockSpec(memory_space=pltpu.SMEM), tok(width), tok(width), tok(width)]
                     + [page_spec(g) for g in range(gp)] * 2,
            out_specs=tok(width),
            scratch_shapes=[pltpu.VMEM((n_heads * tn, width), BF16),
                            pltpu.VMEM((n_heads * tn, width), F32),
                            pltpu.VMEM((n_heads * tn, 1), F32)]),
        out_shape=jax.ShapeDtypeStruct((bsz, tn, width), F32),
        compiler_params=_params(("parallel", "arbitrary")),
        name="sb_sample",
    )(page_table, bias, qkv3, k3, v3, *([cache_kt] * gp), *([cache_vt] * gp))


def _post_kernel(x_ref, ya_ref, yb_ref, yc_ref, gb_ref, wo_ref, g1_ref, b1_ref,
                 wu_ref, wd_ref, g2_ref, b2_ref, o_ref, *, alpha, n_sub, tf):
    tm = x_ref.shape[0]
    sub = tm // n_sub
    dff = wu_ref.shape[1]
    for s in range(n_sub):
        rows = pl.ds(s * sub, sub)
        yb = _rms_norm(yb_ref[rows, :], gb_ref[...])
        y = jnp.concatenate([ya_ref[rows, :], yb, yc_ref[rows, :]], axis=-1).astype(BF16)
        mix = jnp.dot(y, wo_ref[...], preferred_element_type=F32)
        x1 = _layer_norm(alpha * x_ref[rows, :] + mix, g1_ref[...], b1_ref[...])
        xb = x1.astype(BF16)
        hs = []
        for c in range(dff // tf):
            h = jnp.dot(xb, wu_ref[:, c * tf:(c + 1) * tf], preferred_element_type=F32)
            hs.append(jnp.square(jnp.maximum(h, 0.0)).astype(BF16))
        ff = jnp.dot(jnp.concatenate(hs, axis=-1), wd_ref[...], preferred_element_type=F32)
        o_ref[rows, :] = _layer_norm(alpha * x1 + ff, g2_ref[...], b2_ref[...])


def _post(x2d, ya, yb, yc, g_b, wo_bf16, ln1_g, ln1_b, wu_bf16, wd_bf16, ln2_g, ln2_b, alpha, tm):
    m, d = x2d.shape
    dff = wu_bf16.shape[1]
    row = lambda n: pl.BlockSpec((tm, n), lambda i: (i, 0))
    const = lambda shp: pl.BlockSpec(shp, lambda i: (0, 0), pipeline_mode=pl.Buffered(1))
    return pl.pallas_call(
        functools.partial(_post_kernel, alpha=alpha, n_sub=2, tf=min(1024, dff)),
        grid=(m // tm,),
        in_specs=[row(d), row(ya.shape[1]), row(yb.shape[1]), row(yc.shape[1]),
                  const((1, yb.shape[1])), const(wo_bf16.shape), const((1, d)), const((1, d)),
                  const(wu_bf16.shape), const(wd_bf16.shape), const((1, d)), const((1, d))],
        out_specs=row(d),
        out_shape=jax.ShapeDtypeStruct((m, d), F32),
        compiler_params=_params(("parallel",)),
        name="post",
    )(x2d, ya, yb, yc, g_b, wo_bf16, ln1_g, ln1_b, wu_bf16, wd_bf16, ln2_g, ln2_b)


def _block_diag(w):
    h, i, j = w.shape
    eye = jnp.eye(h, dtype=w.dtype)
    return jnp.einsum("hij,hg->higj", w, eye).reshape(h * i, h * j)


def _layer(x, lru_buf, lru_h, ccm_buf, past, kv_stack, layer, depth, lw, dims, tiles):
    bsz, t, d = x.shape
    lru_w, sb_w, ccm_w, alpha = dims
    tm, tc, tq, tk, gp = tiles
    m = bsz * t
    x2d = x.reshape(m, d)
    if past is None:
        pa, qkv, k_stack, v_stack, pc = _in_proj(x, lw["w_in"], lru_w, sb_w, ccm_w, tm, True,
                                                 layer, depth, kv_stack)
    else:
        pa, qkv, k_stack, v_stack, pc = _in_proj(x2d.reshape(1, m, d), lw["w_in"], lru_w, sb_w, ccm_w,
                                                 tm, False, layer, depth, kv_stack)

    ya, lru_buf_new, lru_h_new = _lru(
        pa.reshape(bsz, t, 2 * lru_w), lru_buf, lru_h.reshape(bsz, 1, lru_w),
        lw["lru_conv_w"], lw["lru_conv_b"], lw["lru_wg"], lw["lru_bg"], lw["lru_lam"], lw["g_a"], tc)
    yc, ccm_buf_new = _ccm(
        pc.reshape(bsz, t, 2 * ccm_w), ccm_buf, lw["ccm_conv_w"], lw["ccm_conv_b"],
        lw["ccm_ln_g"], lw["ccm_ln_b"], lw["g_c"], tc)

    qkv3 = qkv.reshape(bsz, t, 3 * sb_w)
    if past is None:
        yb = _sb_prompt(qkv3, lw["sb_bias"], sb_w, tq, tk)
    else:
        cache_kt, cache_vt, page_table = past
        yb = _sb_sample(qkv3, k_stack[layer].reshape(bsz, t, sb_w), v_stack[layer].reshape(bsz, t, sb_w),
                        cache_kt, cache_vt, page_table, lw["sb_bias"], layer, gp)

    x2 = _post(x2d, ya.reshape(m, lru_w), yb.reshape(m, sb_w), yc.reshape(m, ccm_w),
               lw["g_b"], lw["w_out"], lw["ln1_g"], lw["ln1_b"],
               lw["w_up"], lw["w_down"], lw["ln2_g"], lw["ln2_b"], alpha, tm)
    return (x2.reshape(bsz, t, d), (k_stack, v_stack),
            (lru_buf_new, lru_h_new.reshape(bsz, lru_w), ccm_buf_new))


def kernel(x_prompt, x_sample, cache_k, cache_v, page_table, state_lru_conv, state_lru_h, state_ccm_conv,
           w_in, sb_bias, lru_conv_w, lru_conv_b, lru_wa, lru_ba, lru_wx, lru_bx, lru_lam,
           ccm_conv_w, ccm_conv_b, ccm_ln_g, ccm_ln_b, g_mix, w_out,
           ln1_g, ln1_b, w_up, w_down, ln2_g, ln2_b):
    depth = w_in.shape[0]
    bsz, seq, d = x_prompt.shape
    dec_b, dec_t, _ = x_sample.shape
    lru_w = lru_lam.shape[1]
    ccm_w = ccm_ln_g.shape[1]
    sb_w = g_mix.shape[1] - lru_w - ccm_w
    alpha = (2 * depth) ** 0.25
    dims = (lru_w, sb_w, ccm_w, alpha)
    n_pool, page = cache_k.shape[1], cache_k.shape[2]
    cache_kt = jnp.transpose(cache_k, (0, 1, 3, 4, 2)).reshape(depth, n_pool, sb_w, page)
    cache_vt = jnp.transpose(cache_v, (0, 1, 3, 4, 2)).reshape(depth, n_pool, sb_w, page)

    zero_lru_buf = jnp.zeros((bsz, LRU_CONV - 1, lru_w), F32)
    zero_lru_h = jnp.zeros((bsz, lru_w), F32)
    zero_ccm_buf = jnp.zeros((bsz, CCM_CONV - 1, ccm_w), F32)

    prompt_tiles = (min(512, bsz * seq), min(512, seq), 256, 256, 0)
    sample_tiles = (dec_b * dec_t, dec_t, 0, 0, min(16, page_table.shape[1]))

    heads = sb_w // SB_HEAD_DIM
    kv_p = tuple(jnp.zeros((depth, bsz, sb_w, seq), F32) for _ in range(2))
    kv_s = tuple(jnp.zeros((depth, 1, dec_b * dec_t, sb_w), F32) for _ in range(2))

    xp, xs = x_prompt, x_sample
    outs_p, outs_s = [], []
    for l in range(depth):
        row = lambda a: a[l].reshape(1, -1)
        lw = {
            "w_in": w_in[l].astype(BF16), "sb_bias": sb_bias[l],
            "lru_conv_w": lru_conv_w[l], "lru_conv_b": row(lru_conv_b),
            "lru_wg": jnp.concatenate([_block_diag(lru_wa[l]), _block_diag(lru_wx[l])], axis=1).astype(BF16),
            "lru_bg": jnp.concatenate([lru_ba[l], lru_bx[l]]).reshape(1, -1),
            "lru_lam": row(lru_lam),
            "ccm_conv_w": ccm_conv_w[l], "ccm_conv_b": row(ccm_conv_b),
            "ccm_ln_g": row(ccm_ln_g), "ccm_ln_b": row(ccm_ln_b),
            "g_a": g_mix[l, :lru_w].reshape(1, -1),
            "g_b": g_mix[l, lru_w:lru_w + sb_w].reshape(1, -1),
            "g_c": g_mix[l, lru_w + sb_w:].reshape(1, -1),
            "w_out": w_out[l].astype(BF16),
            "ln1_g": row(ln1_g), "ln1_b": row(ln1_b),
            "w_up": w_up[l].astype(BF16), "w_down": w_down[l].astype(BF16),
            "ln2_g": row(ln2_g), "ln2_b": row(ln2_b),
        }
        xp, kv_p, st = _layer(xp, zero_lru_buf, zero_lru_h, zero_ccm_buf, None, kv_p, l, depth,
                              lw, dims, prompt_tiles)
        outs_p.append(st)
        xs, kv_s, st = _layer(xs, state_lru_conv[l], state_lru_h[l], state_ccm_conv[l],
                              (cache_kt, cache_vt, page_table), kv_s, l, depth, lw, dims, sample_tiles)
        outs_s.append(st)

    stack = lambda outs, n: jnp.stack([o[n] for o in outs])
    kv_prompt = [jnp.transpose(a.reshape(depth, bsz, heads, SB_HEAD_DIM, seq), (0, 1, 4, 2, 3)) for a in kv_p]
    kv_sample = [a.reshape(depth, dec_b, dec_t, heads, SB_HEAD_DIM) for a in kv_s]
    return (xp, xs, kv_prompt[0], kv_prompt[1],
            stack(outs_p, 0), stack(outs_p, 1), stack(outs_p, 2),
            kv_sample[0], kv_sample[1],
            stack(outs_s, 0), stack(outs_s, 1), stack(outs_s, 2))
```
